```python
import math
import jax
import jax.numpy as jnp
from jax import lax
import numpy as np

D_MODEL = 1024
BATCH = 8
SEQ = 4096
DEPTH = 2
DEC_BATCH = 2
DEC_SEQ = 8192
PAST_LEN = 128

F32 = jnp.float32
D_PLE = 256
BRANCH_WIDTH = 512
N_BRANCH = 3
DN_HEADS = 4
DN_HEAD_DIM = 128
DN_WIDTH = DN_HEADS * DN_HEAD_DIM
DN_CHUNK = 64
HG_HEADS = 4
HG_HEAD_DIM = 128
HG_WIDTH = HG_HEADS * HG_HEAD_DIM
HG_CHUNK = 64
LB_FLOOR = 1e-30
LRU_WIDTH = 512
LRU_BLOCKS = 8
LRU_BLOCK_DIM = LRU_WIDTH // LRU_BLOCKS
LRU_C = 8.0
CONV_WIDTH = 4
CONV_PAD = (1, 2)
D_FF = 4 * D_MODEL
ALPHA = (2.0 * DEPTH) ** 0.25
OUT_SCALE = (8.0 * DEPTH) ** -0.25
LN_EPS = 1e-5
RMS_EPS = 1e-6
L2_EPS = 1e-6
IN_SPLITS = (DN_WIDTH, DN_WIDTH, DN_WIDTH, DN_WIDTH, DN_HEADS, DN_HEADS, DN_HEADS, DN_HEADS,
             HG_WIDTH, HG_WIDTH, HG_WIDTH, HG_WIDTH, HG_WIDTH,
             LRU_WIDTH, LRU_WIDTH,
             D_MODEL, D_MODEL, D_MODEL)
N_IN = 4 * DN_WIDTH + 4 * DN_HEADS + 5 * HG_WIDTH + 2 * LRU_WIDTH + N_BRANCH * D_MODEL

kernel_name = 'hybrid_bidir_deltanet_hgrn2_rglru_encoder'


def _split_cols(h):
    offs = []
    acc = 0
    for s in IN_SPLITS[:-1]:
        acc += s
        offs.append(acc)
    return jnp.split(h, offs, axis=-1)


def _flip(t):
    return jnp.flip(t, axis=1)


def _layer_norm(x, g, b):
    xf = x.astype(F32)
    mu = jnp.mean(xf, -1, keepdims=True)
    xc = xf - mu
    var = jnp.mean(xc * xc, -1, keepdims=True)
    return (xc * lax.rsqrt(var + LN_EPS) * g.astype(F32) + b.astype(F32)).astype(x.dtype)


def _gated_rms_norm(o, z, w):
    o = o * lax.rsqrt(jnp.mean(o * o, -1, keepdims=True) + RMS_EPS)
    return o * w.astype(F32) * jax.nn.silu(z)


def _l2norm(t):
    return t * lax.rsqrt(jnp.sum(t * t, -1, keepdims=True) + L2_EPS)


def _depthwise_conv(x, w):
    return lax.conv_general_dilated(x, w[:, None, :].astype(x.dtype), window_strides=(1,), padding=[CONV_PAD],
                                    dimension_numbers=('NWC', 'WIO', 'NWC'), feature_group_count=x.shape[-1])


def _to_chunks(t, C):
    B, T, H = t.shape[:3]
    t = t.reshape((B, T // C, C, H) + t.shape[3:])
    return jnp.moveaxis(t, 3, 1)


def _from_chunks(o):
    N, B, H, C, D = o.shape
    return jnp.transpose(o, (1, 0, 3, 2, 4)).reshape(B, N * C, H, D)


def _masked_exp(diff, mask):
    return jnp.where(mask, jnp.exp(jnp.where(mask, diff, 0.0)), 0.0)


def _gated_delta_chunked(q, k, v, g, beta):
    B, T, H, Dk = q.shape
    Dv = v.shape[-1]
    C = DN_CHUNK
    q, k, v = _to_chunks(q, C), _to_chunks(k, C), _to_chunks(v, C)
    g, beta = _to_chunks(g, C), _to_chunks(beta, C)
    g = jnp.cumsum(g, axis=-1)
    causal = jnp.tril(jnp.ones((C, C), bool))
    strict = jnp.tril(jnp.ones((C, C), bool), -1)
    decay = _masked_exp(g[..., :, None] - g[..., None, :], causal)
    k_beta = k * beta[..., None]
    v_beta = v * beta[..., None]
    L = jnp.where(strict, jnp.einsum('bhnid,bhnjd->bhnij', k_beta, k) * decay, 0.0)
    eye = jnp.eye(C, dtype=q.dtype)
    t_inv = lax.linalg.triangular_solve(eye + L, jnp.broadcast_to(eye, L.shape), left_side=True, lower=True)
    u = t_inv @ v_beta
    w = t_inv @ (k_beta * jnp.exp(g)[..., None])
    attn = jnp.einsum('bhnid,bhnjd->bhnij', q, k) * decay
    q_dec = q * jnp.exp(g)[..., None]
    k_dec = k * jnp.exp(g[..., -1:] - g)[..., None]
    g_last = jnp.exp(g[..., -1])

    def step(S, inp):
        u_n, w_n, attn_n, qd_n, kd_n, gl_n = inp
        v_new = u_n - w_n @ S
        o = qd_n @ S + attn_n @ v_new
        S = S * gl_n[..., None, None] + jnp.swapaxes(kd_n, -1, -2) @ v_new
        return S, o

    xs = (jnp.moveaxis(u, 2, 0), jnp.moveaxis(w, 2, 0), jnp.moveaxis(attn, 2, 0),
          jnp.moveaxis(q_dec, 2, 0), jnp.moveaxis(k_dec, 2, 0), jnp.moveaxis(g_last, 2, 0))
    S0 = jnp.zeros((B, H, Dk, Dv), q.dtype)
    _, o = lax.scan(step, S0, xs)
    return _from_chunks(o)


def _hgrn2_chunked(q, k, v, log_f):
    B, T, H, Dk = q.shape
    Dv = v.shape[-1]
    C = HG_CHUNK
    q, k, v = _to_chunks(q, C), _to_chunks(k, C), _to_chunks(v, C)
    b = jnp.cumsum(_to_chunks(log_f, C), axis=3)
    causal = jnp.tril(jnp.ones((C, C), bool))[:, :, None]

    def step(S, inp):
        q_n, k_n, v_n, b_n = inp
        dec = _masked_exp(b_n[..., :, None, :] - b_n[..., None, :, :], causal)
        A = jnp.einsum('bhid,bhjd,bhijd->bhij', q_n, k_n, dec)
        o = (q_n * jnp.exp(b_n)) @ S + A @ v_n
        S = (jnp.exp(b_n[..., -1, :])[..., None] * S
             + jnp.swapaxes(k_n * jnp.exp(b_n[..., -1:, :] - b_n), -1, -2) @ v_n)
        return S, o

    xs = (jnp.moveaxis(q, 2, 0), jnp.moveaxis(k, 2, 0), jnp.moveaxis(v, 2, 0), jnp.moveaxis(b, 2, 0))
    S0 = jnp.zeros((B, H, Dk, Dv), q.dtype)
    _, o = lax.scan(step, S0, xs)
    return _from_chunks(o)


def _lin_combine(left, right):
    a1, b1 = left
    a2, b2 = right
    return a1 * a2, a2 * b1 + b2


def _deltanet_branch(q, k, v, z, a_f, a_b, b_f, b_b, conv_w, A_log, dt_bias, norm_w):
    B, T, _ = q.shape
    qkv = jax.nn.silu(_depthwise_conv(jnp.concatenate([q, k, v], axis=-1), conv_w).astype(F32))
    q, k, v = jnp.split(qkv, 3, axis=-1)

    def heads(t):
        return t.astype(F32).reshape(B, T, DN_HEADS, DN_HEAD_DIM)

    q = _l2norm(heads(q)) * (DN_HEAD_DIM ** -0.5)
    k = _l2norm(heads(k))
    v = heads(v)

    def log_decay(a, d):
        return -jnp.exp(A_log[d].astype(F32)) * jax.nn.softplus(a.astype(F32) + dt_bias[d].astype(F32))

    g_f, g_b = log_decay(a_f, 0), log_decay(a_b, 1)
    beta_f, beta_b = jax.nn.sigmoid(b_f.astype(F32)), jax.nn.sigmoid(b_b.astype(F32))
    o_f = _gated_delta_chunked(q, k, v, g_f, beta_f)
    o_b = _flip(_gated_delta_chunked(_flip(q), _flip(k), _flip(v), _flip(g_b), _flip(beta_b)))
    o = _gated_rms_norm(o_f + o_b, heads(z), norm_w)
    return o.reshape(B, T, DN_WIDTH)


def _hgrn2_branch(q, f_f, f_b, i, g, lb, norm_w):
    B, T, _ = q.shape

    def heads(t):
        return t.astype(F32).reshape(B, T, HG_HEADS, HG_HEAD_DIM)

    log_lb = jnp.log(jnp.maximum(lb, LB_FLOOR))

    def gates(fz):
        fz = fz.astype(F32)
        log_f = jnp.logaddexp(jax.nn.log_sigmoid(fz), log_lb + jax.nn.log_sigmoid(-fz))
        k = (1.0 - lb) * jax.nn.sigmoid(-fz)
        return heads(log_f), heads(k)

    q, v = heads(q), heads(i)
    lf_f, k_f = gates(f_f)
    lf_b, k_b = gates(f_b)
    o = _hgrn2_chunked(q, k_f, v, lf_f) + _flip(_hgrn2_chunked(_flip(q), _flip(k_b), _flip(v), _flip(lf_b)))
    o = _gated_rms_norm(o, heads(g), norm_w)
    return o.reshape(B, T, HG_WIDTH)


def _rglru_branch(xc, gate, conv_w, conv_b, wa, ba, wx, bx, lam):
    B, T, _ = xc.shape
    xc = (_depthwise_conv(xc, conv_w) + conv_b).astype(F32)
    xb = xc.reshape(B, T, LRU_BLOCKS, LRU_BLOCK_DIM)

    def block_diag(w, b):
        return jnp.einsum('btki,kij->btkj', xb, w.astype(F32)).reshape(B, T, LRU_WIDTH) + b.astype(F32)

    def direction(d, reverse):
        r = jax.nn.sigmoid(block_diag(wa[d], ba[d]))
        i = jax.nn.sigmoid(block_diag(wx[d], bx[d]))
        log_a = LRU_C * r * jax.nn.log_sigmoid(lam[d].astype(F32))
        a = jnp.exp(log_a)
        u = jnp.sqrt(jnp.maximum(-jnp.expm1(2.0 * log_a), 0.0)) * (i * xc)
        _, h = lax.associative_scan(_lin_combine, (a, u), reverse=reverse, axis=1)
        return h

    h = direction(0, False) + direction(1, True)
    return h * jax.nn.gelu(gate.astype(F32))


def _layer(x, p_l, lb, l, W):
    h = jnp.einsum('btd,de->bte', x, W['w_in'][l])
    (dq, dk, dv, dz, da_f, da_b, db_f, db_b,
     hq, hf_f, hf_b, hi, hgt,
     cx, cg, gA, gB, gC) = _split_cols(h)
    o_dn = _deltanet_branch(dq, dk, dv, dz, da_f, da_b, db_f, db_b,
                            W['dn_conv_w'][l], W['dn_A_log'][l], W['dn_dt_bias'][l], W['dn_norm_w'][l])
    o_hg = _hgrn2_branch(hq, hf_f, hf_b, hi, hgt, lb, W['hg_norm_w'][l])
    o_lru = _rglru_branch(cx, cg, W['lru_conv_w'][l], W['lru_conv_b'][l], W['lru_wa'][l], W['lru_ba'][l],
                          W['lru_wx'][l], W['lru_bx'][l], W['lru_lambda'][l])
    ob = jnp.stack([o_dn, o_hg, o_lru], axis=2).astype(x.dtype)
    branches = jnp.einsum('btnc,ncd->btnd', ob, W['w_branch'][l])
    gates = jax.nn.sigmoid(jnp.stack([gA, gB, gC], axis=2))
    mix = jnp.einsum('btd,de->bte', jnp.sum(gates * branches, axis=2), W['w_out'][l])
    x1 = _layer_norm(ALPHA * x + mix, W['ln1_g'][l], W['ln1_b'][l])
    ff = jnp.einsum('btf,fd->btd', jnp.square(jax.nn.relu(jnp.einsum('btd,df->btf', x1, W['w_mlp1'][l]))),
                    W['w_mlp2'][l])
    ple = (jax.nn.sigmoid(jnp.einsum('btd,de->bte', x1, W['w_ple_gate'][l]))
           * jnp.einsum('btp,pd->btd', p_l, W['w_ple_proj'][l]))
    return _layer_norm(ALPHA * x1 + ff + ple, W['ln2_g'][l], W['ln2_b'][l])


def _trunk(x, p, W):
    sm = jax.nn.softmax(W['hg_lb_logits'].astype(F32), axis=0)
    lbs = jnp.maximum(jnp.cumsum(sm, axis=0) - sm[0], 0.0)
    x = _layer_norm(x, W['emb_ln_g'], W['emb_ln_b'])
    for l in range(DEPTH):
        x = _layer(x, p[l], lbs[l], l, W)
    return x


def setup_inputs(seed: int = 0) -> dict:
    key = jax.random.key(seed)
    ks = iter(jax.random.split(key, 32))
    L = DEPTH

    def nrm(shape, scale):
        return jax.random.normal(next(ks), shape, F32) * scale

    x_prompt = nrm((BATCH, SEQ, D_MODEL), 1.0)
    x_sample = nrm((DEC_BATCH, DEC_SEQ, D_MODEL), 1.0)
    p_prompt = nrm((DEPTH, BATCH, SEQ, D_PLE), 1.0)
    p_sample = nrm((DEPTH, DEC_BATCH, DEC_SEQ, D_PLE), 1.0)
    emb_ln_g = 1.0 + nrm((D_MODEL,), 0.02)
    emb_ln_b = nrm((D_MODEL,), 0.02)
    w_in = nrm((L, D_MODEL, N_IN), D_MODEL ** -0.5)
    dn_conv_w = nrm((L, CONV_WIDTH, 3 * DN_WIDTH), CONV_WIDTH ** -0.5)
    dn_A_log = jnp.log(jax.random.uniform(next(ks), (L, 2, DN_HEADS), F32, 1.0, 16.0))
    dt = jnp.exp(jax.random.uniform(next(ks), (L, 2, DN_HEADS), F32, math.log(1e-3), math.log(1e-1)))
    dn_dt_bias = dt + jnp.log(-jnp.expm1(-dt))
    dn_norm_w = 1.0 + nrm((L, DN_HEAD_DIM), 0.02)
    hg_lb_logits = nrm((L, HG_WIDTH), 0.5)
    hg_norm_w = 1.0 + nrm((L, HG_HEAD_DIM), 0.02)
    lru_conv_w = nrm((L, CONV_WIDTH, LRU_WIDTH), CONV_WIDTH ** -0.5)
    lru_conv_b = nrm((L, LRU_WIDTH), 0.02)
    lru_wa = nrm((L, 2, LRU_BLOCKS, LRU_BLOCK_DIM, LRU_BLOCK_DIM), LRU_BLOCK_DIM ** -0.5)
    lru_ba = nrm((L, 2, LRU_WIDTH), 0.02)
    lru_wx = nrm((L, 2, LRU_BLOCKS, LRU_BLOCK_DIM, LRU_BLOCK_DIM), LRU_BLOCK_DIM ** -0.5)
    lru_bx = nrm((L, 2, LRU_WIDTH), 0.02)
    a_c = jax.random.uniform(next(ks), (L, 2, LRU_WIDTH), F32, 0.9, 0.999)
    s = a_c ** (1.0 / LRU_C)
    lru_lambda = jnp.log(s) - jnp.log1p(-s)
    w_branch = nrm((L, N_BRANCH, BRANCH_WIDTH, D_MODEL), BRANCH_WIDTH ** -0.5)
    w_out = nrm((L, D_MODEL, D_MODEL), OUT_SCALE * D_MODEL ** -0.5)
    ln1_g = 1.0 + nrm((L, D_MODEL), 0.02)
    ln1_b = nrm((L, D_MODEL), 0.02)
    ln2_g = 1.0 + nrm((L, D_MODEL), 0.02)
    ln2_b = nrm((L, D_MODEL), 0.02)
    w_mlp1 = nrm((L, D_MODEL, D_FF), D_MODEL ** -0.5)
    w_mlp2 = nrm((L, D_FF, D_MODEL), OUT_SCALE * D_FF ** -0.5)
    w_ple_gate = nrm((L, D_MODEL, D_MODEL), D_MODEL ** -0.5)
    w_ple_proj = nrm((L, D_PLE, D_MODEL), OUT_SCALE * D_PLE ** -0.5)
    return {'x_prompt': x_prompt, 'x_sample': x_sample, 'p_prompt': p_prompt, 'p_sample': p_sample,
            'emb_ln_g': emb_ln_g, 'emb_ln_b': emb_ln_b, 'w_in': w_in,
            'dn_conv_w': dn_conv_w, 'dn_A_log': dn_A_log, 'dn_dt_bias': dn_dt_bias, 'dn_norm_w': dn_norm_w,
            'hg_lb_logits': hg_lb_logits, 'hg_norm_w': hg_norm_w,
            'lru_conv_w': lru_conv_w, 'lru_conv_b': lru_conv_b, 'lru_wa': lru_wa, 'lru_ba': lru_ba,
            'lru_wx': lru_wx, 'lru_bx': lru_bx, 'lru_lambda': lru_lambda,
            'w_branch': w_branch, 'w_out': w_out,
            'ln1_g': ln1_g, 'ln1_b': ln1_b, 'ln2_g': ln2_g, 'ln2_b': ln2_b,
            'w_mlp1': w_mlp1, 'w_mlp2': w_mlp2, 'w_ple_gate': w_ple_gate, 'w_ple_proj': w_ple_proj}


def reference(x_prompt, x_sample, p_prompt, p_sample, emb_ln_g, emb_ln_b, w_in,
              dn_conv_w, dn_A_log, dn_dt_bias, dn_norm_w, hg_lb_logits, hg_norm_w,
              lru_conv_w, lru_conv_b, lru_wa, lru_ba, lru_wx, lru_bx, lru_lambda,
              w_branch, w_out, ln1_g, ln1_b, ln2_g, ln2_b, w_mlp1, w_mlp2, w_ple_gate, w_ple_proj):
    W = dict(emb_ln_g=emb_ln_g, emb_ln_b=emb_ln_b, w_in=w_in,
             dn_conv_w=dn_conv_w, dn_A_log=dn_A_log, dn_dt_bias=dn_dt_bias, dn_norm_w=dn_norm_w,
             hg_lb_logits=hg_lb_logits, hg_norm_w=hg_norm_w,
             lru_conv_w=lru_conv_w, lru_conv_b=lru_conv_b, lru_wa=lru_wa, lru_ba=lru_ba,
             lru_wx=lru_wx, lru_bx=lru_bx, lru_lambda=lru_lambda,
             w_branch=w_branch, w_out=w_out, ln1_g=ln1_g, ln1_b=ln1_b, ln2_g=ln2_g, ln2_b=ln2_b,
             w_mlp1=w_mlp1, w_mlp2=w_mlp2, w_ple_gate=w_ple_gate, w_ple_proj=w_ple_proj)
    y_prompt = _trunk(x_prompt, p_prompt, W)
    y_sample = _trunk(x_sample, p_sample, W)
    return (y_prompt, y_sample)
```

```python
import functools

import jax
import jax.numpy as jnp
from jax import lax
from jax.experimental import pallas as pl
from jax.experimental.pallas import tpu as pltpu

F32 = jnp.float32
BF16 = jnp.bfloat16

D_MODEL = 1024
D_PLE = 256
N_HEADS = 4
HEAD_DIM = 128
WIDTH = N_HEADS * HEAD_DIM
CHUNK = 64
SUB = 16
LB_FLOOR = 1e-30
LRU_BLOCKS = 8
LRU_C = 8.0
CONV_WIDTH = 4
D_FF = 4 * D_MODEL
LN_EPS = 1e-5
RMS_EPS = 1e-6
L2_EPS = 1e-6
HALO = 8
NARROW = 128

VMEM_LIMIT_BYTES = 56 * 1024 * 1024
TOKEN_BLOCK = 256
TIME_BLOCK = 256

HIGHEST = lax.Precision.HIGHEST


def _dot(a, b, precision=None):
    return jnp.dot(a, b, preferred_element_type=F32, precision=precision)


def _dot_nt(a, b):
    return lax.dot_general(a, b, (((1,), (1,)), ((), ())), preferred_element_type=F32)


def _bf(x):
    return x.astype(BF16)


def _layer_norm(x, g, b):
    mu = jnp.mean(x, -1, keepdims=True)
    xc = x - mu
    var = jnp.mean(xc * xc, -1, keepdims=True)
    return xc * lax.rsqrt(var + LN_EPS) * g + b


def _softplus(x):
    return jnp.maximum(x, 0.0) + jnp.log1p(jnp.exp(-jnp.abs(x)))


def _log_sigmoid(x):
    return jnp.minimum(x, 0.0) - jnp.log1p(jnp.exp(-jnp.abs(x)))


def _logaddexp(a, b):
    return jnp.maximum(a, b) + jnp.log1p(jnp.exp(-jnp.abs(a - b)))


def _silu(x):
    return x * jax.nn.sigmoid(x)


def _const_spec(shape):
    zeros = (0,) * len(shape)
    return pl.BlockSpec(shape, lambda *_: zeros, pipeline_mode=pl.Buffered(1))


def _params(n_axes):
    return pltpu.CompilerParams(dimension_semantics=("arbitrary",) * n_axes,
                                vmem_limit_bytes=VMEM_LIMIT_BYTES)


def _in_proj_kernel(x_ref, g_ref, b_ref, w_ref, wn_ref, *out_refs, apply_ln):
    x = x_ref[...]
    if apply_ln:
        xo_ref, h_ref, hn_ref = out_refs
        x = _layer_norm(x, g_ref[...], b_ref[...])
        xo_ref[...] = x
    else:
        h_ref, hn_ref = out_refs
    xb = _bf(x)
    h_ref[...] = _dot(xb, w_ref[...])
    hn_ref[...] = _dot(xb, wn_ref[...])


def _in_proj(x, g, b, w_wide, w_narrow, apply_ln):
    n = x.shape[0]
    tm = TOKEN_BLOCK
    wide = w_wide.shape[1]
    row = lambda width: pl.BlockSpec((tm, width), lambda i: (i, 0))
    out_shape = [jax.ShapeDtypeStruct((n, wide), F32), jax.ShapeDtypeStruct((n, NARROW), F32)]
    out_specs = [row(wide), row(NARROW)]
    if apply_ln:
        out_shape = [jax.ShapeDtypeStruct((n, D_MODEL), F32)] + out_shape
        out_specs = [row(D_MODEL)] + out_specs
    return pl.pallas_call(
        functools.partial(_in_proj_kernel, apply_ln=apply_ln),
        grid=(n // tm,),
        in_specs=[row(D_MODEL), _const_spec((1, D_MODEL)), _const_spec((1, D_MODEL)),
                  _const_spec(w_wide.shape), _const_spec(w_narrow.shape)],
        out_specs=out_specs, out_shape=out_shape,
        compiler_params=_params(1), name="in_proj",
    )(x, g, b, w_wide, w_narrow)


def _time_index(rev, nt):
    t = pl.program_id(1)
    return (nt - 1 - t) if rev else t


def _fill_halo(xe_ref, main_ref, prev_ref, next_ref, tt, nt, tb):
    xe_ref[0:HALO, :] = jnp.where(tt > 0, prev_ref[0], 0.0)
    xe_ref[HALO:HALO + tb, :] = main_ref[0]
    xe_ref[HALO + tb:2 * HALO + tb, :] = jnp.where(tt < nt - 1, next_ref[0], 0.0)


def _conv4(xe_ref, cw, tb):
    acc = cw[0:1, :] * xe_ref[pl.ds(HALO - 1, tb), :]
    for j in range(1, CONV_WIDTH):
        acc = acc + cw[j:j + 1, :] * xe_ref[pl.ds(HALO - 1 + j, tb), :]
    return acc


def _chunk_cumsum(x, rev):
    tb = x.shape[0]
    rc = lax.broadcasted_iota(jnp.int32, x.shape, 0) % CHUNK
    s = 1
    while s < CHUNK:
        if rev:
            x = x + jnp.where(rc < CHUNK - s, pltpu.roll(x, tb - s, 0), 0.0)
        else:
            x = x + jnp.where(rc >= s, pltpu.roll(x, s, 0), 0.0)
        s *= 2
    return x


def _halo_specs(width, col_block, tb, nt, t_total, rev):
    per = tb // HALO
    last = t_total // HALO - 1

    def tidx(t):
        return (nt - 1 - t) if rev else t

    main = pl.BlockSpec((1, tb, width), lambda b, t: (b, tidx(t), col_block))
    prev = pl.BlockSpec((1, HALO, width), lambda b, t: (b, jnp.maximum(tidx(t) * per - 1, 0), col_block))
    nxt = pl.BlockSpec((1, HALO, width), lambda b, t: (b, jnp.minimum((tidx(t) + 1) * per, last), col_block))
    return main, prev, nxt


def _dn_kernel(main_ref, prev_ref, next_ref, ab_ref, cw_ref, alog_ref, dtb_ref, o_ref,
               xe_ref, q_s, k_s, v_s, gc_s, beta_s, st_ref, *, rev, tb, nt):
    tt = _time_index(rev, nt)

    @pl.when(pl.program_id(1) == 0)
    def _():
        st_ref[...] = jnp.zeros_like(st_ref)

    _fill_halo(xe_ref, main_ref, prev_ref, next_ref, tt, nt, tb)
    qkv = _silu(_conv4(xe_ref, cw_ref[...], tb))
    for h in range(N_HEADS):
        lo = h * HEAD_DIM
        qh = qkv[:, lo:lo + HEAD_DIM]
        kh = qkv[:, WIDTH + lo:WIDTH + lo + HEAD_DIM]
        q_s[:, lo:lo + HEAD_DIM] = qh * (lax.rsqrt(jnp.sum(qh * qh, -1, keepdims=True) + L2_EPS)
                                         * (HEAD_DIM ** -0.5))
        k_s[:, lo:lo + HEAD_DIM] = kh * lax.rsqrt(jnp.sum(kh * kh, -1, keepdims=True) + L2_EPS)
    v_s[...] = qkv[:, 2 * WIDTH:3 * WIDTH]

    ab = ab_ref[0]
    g = -jnp.exp(alog_ref[...]) * _softplus(ab + dtb_ref[...])
    beta_s[...] = jax.nn.sigmoid(ab)
    gc_s[...] = _chunk_cumsum(g, rev)

    ii = lax.broadcasted_iota(jnp.int32, (CHUNK, CHUNK), 0)
    jj = lax.broadcasted_iota(jnp.int32, (CHUNK, CHUNK), 1)
    incl = (ii <= jj) if rev else (ii >= jj)
    strict = (ii < jj) if rev else (ii > jj)
    eye = (ii == jj).astype(F32)
    last = 0 if rev else CHUNK - 1
    nc = tb // CHUNK

    def chunk(c, carry):
        cc = (nc - 1 - c) if rev else c
        r0 = pl.multiple_of(cc * CHUNK, CHUNK)
        gc = gc_s[pl.ds(r0, CHUNK), :]
        gct = gc.T
        bt = beta_s[pl.ds(r0, CHUNK), :]
        for h in range(N_HEADS):
            lo = h * HEAD_DIM
            lane_g = (N_HEADS if rev else 0) + h
            lane_b = 2 * N_HEADS + lane_g
            gcol = gc[:, lane_g:lane_g + 1]
            grow = gct[lane_g:lane_g + 1, :]
            bcol = bt[:, lane_b:lane_b + 1]
            q = q_s[pl.ds(r0, CHUNK), lo:lo + HEAD_DIM]
            k = k_s[pl.ds(r0, CHUNK), lo:lo + HEAD_DIM]
            v = v_s[pl.ds(r0, CHUNK), lo:lo + HEAD_DIM]
            dec = jnp.where(incl, jnp.exp(jnp.where(incl, gcol - grow, 0.0)), 0.0)
            kb = k * bcol
            qk = _dot_nt(_bf(jnp.concatenate([q, kb], axis=0)), _bf(k))
            attn = qk[:CHUNK] * dec
            lmat = jnp.where(strict, qk[CHUNK:] * dec, 0.0)
            pw = -lmat
            tinv = eye + pw
            for _ in range(5):
                pw = _dot(pw, pw, HIGHEST)
                tinv = tinv + _dot(tinv, pw, HIGHEST)
            eg = jnp.exp(gcol)
            uw = _dot(_bf(tinv), _bf(jnp.concatenate([v * bcol, kb * eg], axis=1)))
            s_h = st_ref[h]
            wq = _dot(_bf(jnp.concatenate([uw[:, HEAD_DIM:], q * eg], axis=0)), _bf(s_h))
            v_new = uw[:, :HEAD_DIM] - wq[:CHUNK]
            o = wq[CHUNK:] + _dot(_bf(attn), _bf(v_new))
            glast = gc[last:last + 1, lane_g:lane_g + 1]
            kdec = k * jnp.exp(glast - gcol)
            st_ref[h] = s_h * jnp.exp(glast) + _dot(_bf(kdec.T), _bf(v_new))
            o_ref[0, pl.ds(r0, CHUNK), lo:lo + HEAD_DIM] = o
        return carry

    lax.fori_loop(0, nc, chunk, 0)


def _deltanet(h_wide, h_narrow, conv_w, alog_vec, dtb_vec, rev):
    bsz, t_total, _ = h_wide.shape
    tb = TIME_BLOCK
    nt = t_total // tb
    main, prev, nxt = _halo_specs(3 * WIDTH, 0, tb, nt, t_total, rev)
    tidx = (lambda t: nt - 1 - t) if rev else (lambda t: t)
    return pl.pallas_call(
        functools.partial(_dn_kernel, rev=rev, tb=tb, nt=nt),
        grid=(bsz, nt),
        in_specs=[main, prev, nxt,
                  pl.BlockSpec((1, tb, NARROW), lambda b, t: (b, tidx(t), 0)),
                  _const_spec(conv_w.shape), _const_spec((1, NARROW)), _const_spec((1, NARROW))],
        out_specs=pl.BlockSpec((1, tb, WIDTH), lambda b, t: (b, tidx(t), 0)),
        out_shape=jax.ShapeDtypeStruct((bsz, t_total, WIDTH), F32),
        scratch_shapes=[pltpu.VMEM((tb + 2 * HALO, 3 * WIDTH), F32),
                        pltpu.VMEM((tb, WIDTH), F32), pltpu.VMEM((tb, WIDTH), F32), pltpu.VMEM((tb, WIDTH), F32),
                        pltpu.VMEM((tb, NARROW), F32), pltpu.VMEM((tb, NARROW), F32),
                        pltpu.VMEM((N_HEADS, HEAD_DIM, HEAD_DIM), F32)],
        compiler_params=_params(2), name="deltanet_rev" if rev else "deltanet_fwd",
    )(h_wide, h_wide, h_wide, h_narrow, conv_w, alog_vec, dtb_vec)


def _hg_kernel(q_ref, f_ref, i_ref, lg_ref, o_ref, b_s, k_s, st_ref, *, rev, tb, nt, layer):
    @pl.when(pl.program_id(1) == 0)
    def _():
        st_ref[...] = jnp.zeros_like(st_ref)

    lg = lg_ref[...]
    ex = jnp.exp(lg - jnp.max(lg, axis=0, keepdims=True))
    sm = ex / jnp.sum(ex, axis=0, keepdims=True)
    lb = jnp.maximum(jnp.sum(sm[0:layer + 1], axis=0, keepdims=True) - sm[0:1], 0.0)
    log_lb = jnp.log(jnp.maximum(lb, LB_FLOOR))

    fz = f_ref[0]
    log_f = _logaddexp(_log_sigmoid(fz), log_lb + _log_sigmoid(-fz))
    k_s[...] = (1.0 - lb) * jax.nn.sigmoid(-fz)
    b_s[...] = _chunk_cumsum(log_f, rev)

    rows = lax.broadcasted_iota(jnp.int32, (CHUNK, HEAD_DIM), 0)
    sub_i = lax.broadcasted_iota(jnp.int32, (SUB, HEAD_DIM), 0)
    lane_c = lax.broadcasted_iota(jnp.int32, (SUB, CHUNK), 1)
    last = 0 if rev else CHUNK - 1
    nc = tb // CHUNK
    n_sub = CHUNK // SUB

    def chunk(c, carry):
        cc = (nc - 1 - c) if rev else c
        r0 = pl.multiple_of(cc * CHUNK, CHUNK)
        for h in range(N_HEADS):
            lo = h * HEAD_DIM
            b = b_s[pl.ds(r0, CHUNK), lo:lo + HEAD_DIM]
            q = q_ref[0, pl.ds(r0, CHUNK), lo:lo + HEAD_DIM]
            k = k_s[pl.ds(r0, CHUNK), lo:lo + HEAD_DIM]
            v = i_ref[0, pl.ds(r0, CHUNK), lo:lo + HEAD_DIM]
            blast = b[last:last + 1]
            st = st_ref[h]
            o = _dot_nt(_bf(q * jnp.exp(b)), _bf(st))
            k_state = k * jnp.exp(blast - b)
            st_ref[h] = st * jnp.exp(blast) + _dot(_bf(v.T), _bf(k_state))
            a_rows = []
            for blk in range(n_sub):
                a0 = blk * SUB
                b_blk = b[a0:a0 + SUB]
                q_blk = q[a0:a0 + SUB]
                has_past = (blk < n_sub - 1) if rev else (blk > 0)
                if has_past:
                    if rev:
                        bs = b[a0 + SUB:a0 + SUB + 1]
                        past = rows >= a0 + SUB
                    else:
                        bs = b[a0 - 1:a0]
                        past = rows < a0
                    k_past = jnp.where(past, k * jnp.exp(jnp.where(past, bs - b, 0.0)), 0.0)
                    p = _dot_nt(_bf(q_blk * jnp.exp(b_blk - bs)), _bf(k_past))
                else:
                    p = jnp.zeros((SUB, CHUNK), F32)
                for j in range(SUB):
                    valid = (sub_i <= j) if rev else (sub_i >= j)
                    e = jnp.exp(jnp.where(valid, b_blk - b[a0 + j:a0 + j + 1], 0.0))
                    col = jnp.sum(jnp.where(valid, q_blk * k[a0 + j:a0 + j + 1] * e, 0.0), -1, keepdims=True)
                    p = jnp.where(lane_c == a0 + j, col, p)
                a_rows.append(p)
            a = jnp.concatenate(a_rows, axis=0)
            o_ref[0, pl.ds(r0, CHUNK), lo:lo + HEAD_DIM] = o + _dot(_bf(a), _bf(v))
        return carry

    lax.fori_loop(0, nc, chunk, 0)


def _hgrn2(h_wide, lb_logits, layer, rev):
    bsz, t_total, _ = h_wide.shape
    tb = TIME_BLOCK
    nt = t_total // tb
    tidx = (lambda t: nt - 1 - t) if rev else (lambda t: t)
    base = 3 * WIDTH // WIDTH
    col = lambda c: pl.BlockSpec((1, tb, WIDTH), lambda b, t: (b, tidx(t), c))
    return pl.pallas_call(
        functools.partial(_hg_kernel, rev=rev, tb=tb, nt=nt, layer=layer),
        grid=(bsz, nt),
        in_specs=[col(base), col(base + (2 if rev else 1)), col(base + 3), _const_spec(lb_logits.shape)],
        out_specs=pl.BlockSpec((1, tb, WIDTH), lambda b, t: (b, tidx(t), 0)),
        out_shape=jax.ShapeDtypeStruct((bsz, t_total, WIDTH), F32),
        scratch_shapes=[pltpu.VMEM((tb, WIDTH), F32), pltpu.VMEM((tb, WIDTH), F32),
                        pltpu.VMEM((N_HEADS, HEAD_DIM, HEAD_DIM), F32)],
        compiler_params=_params(2), name="hgrn2_rev" if rev else "hgrn2_fwd",
    )(h_wide, h_wide, h_wide, lb_logits)


def _lru_kernel(main_ref, prev_ref, next_ref, cw_ref, cb_ref, wg_ref, bg_ref, lam_ref, o_ref,
                xe_ref, carry_ref, *, rev, tb, nt):
    tt = _time_index(rev, nt)

    @pl.when(pl.program_id(1) == 0)
    def _():
        carry_ref[...] = jnp.zeros_like(carry_ref)

    _fill_halo(xe_ref, main_ref, prev_ref, next_ref, tt, nt, tb)
    xc = _conv4(xe_ref, cw_ref[...], tb) + cb_ref[...]
    gates = jax.nn.sigmoid(_dot(_bf(xc), wg_ref[...]) + bg_ref[...])
    log_a = LRU_C * gates[:, :WIDTH] * _log_sigmoid(lam_ref[...])
    a = jnp.exp(log_a)
    hval = jnp.sqrt(jnp.maximum(1.0 - jnp.exp(2.0 * log_a), 0.0)) * (gates[:, WIDTH:] * xc)

    row = lax.broadcasted_iota(jnp.int32, (tb, WIDTH), 0)
    s = 1
    while s < tb:
        if rev:
            valid = row < tb - s
            shift = tb - s
        else:
            valid = row >= s
            shift = s
        h_sh = pltpu.roll(hval, shift, 0)
        a_sh = pltpu.roll(a, shift, 0)
        hval = jnp.where(valid, a * h_sh + hval, hval)
        a = jnp.where(valid, a * a_sh, a)
        s *= 2
    hval = hval + a * carry_ref[...]
    last = 0 if rev else tb - 1
    carry_ref[...] = hval[last:last + 1]
    o_ref[0] = hval


def _rglru(h_wide, conv_w, conv_b, w_gate, b_gate, lam, rev):
    bsz, t_total, _ = h_wide.shape
    tb = TIME_BLOCK
    nt = t_total // tb
    col_block = h_wide.shape[2] // WIDTH - 1
    main, prev, nxt = _halo_specs(WIDTH, col_block, tb, nt, t_total, rev)
    tidx = (lambda t: nt - 1 - t) if rev else (lambda t: t)
    return pl.pallas_call(
        functools.partial(_lru_kernel, rev=rev, tb=tb, nt=nt),
        grid=(bsz, nt),
        in_specs=[main, prev, nxt, _const_spec(conv_w.shape), _const_spec(conv_b.shape),
                  _const_spec(w_gate.shape), _const_spec(b_gate.shape), _const_spec(lam.shape)],
        out_specs=pl.BlockSpec((1, tb, WIDTH), lambda b, t: (b, tidx(t), 0)),
        out_shape=jax.ShapeDtypeStruct((bsz, t_total, WIDTH), F32),
        scratch_shapes=[pltpu.VMEM((tb + 2 * HALO, WIDTH), F32), pltpu.VMEM((1, WIDTH), F32)],
        compiler_params=_params(2), name="rglru_rev" if rev else "rglru_fwd",
    )(h_wide, h_wide, h_wide, conv_w, conv_b, w_gate, b_gate, lam)


def _head_rms(o, w, gate):
    parts = []
    for h in range(N_HEADS):
        seg = o[:, h * HEAD_DIM:(h + 1) * HEAD_DIM]
        parts.append(seg * lax.rsqrt(jnp.mean(seg * seg, -1, keepdims=True) + RMS_EPS))
    return jnp.concatenate(parts, axis=1) * w * _silu(gate)


def _merge_kernel(x_ref, dnf_ref, dnb_ref, hgf_ref, hgb_ref, luf_ref, lub_ref, wc_ref, dnw_ref, hgw_ref,
                  wb_ref, wo_ref, g_ref, b_ref, o_ref, *, alpha):
    x = x_ref[...]
    hc = _dot(_bf(x), wc_ref[...])
    o_dn = _head_rms(dnf_ref[...] + dnb_ref[...], dnw_ref[...], hc[:, 0:WIDTH])
    o_hg = _head_rms(hgf_ref[...] + hgb_ref[...], hgw_ref[...], hc[:, WIDTH:2 * WIDTH])
    o_lru = (luf_ref[...] + lub_ref[...]) * jax.nn.gelu(hc[:, 2 * WIDTH:3 * WIDTH])
    mixed = None
    for n, ob in enumerate((o_dn, o_hg, o_lru)):
        lo = 3 * WIDTH + n * D_MODEL
        term = jax.nn.sigmoid(hc[:, lo:lo + D_MODEL]) * _dot(_bf(ob), wb_ref[n])
        mixed = term if mixed is None else mixed + term
    mix = _dot(_bf(mixed), wo_ref[...])
    o_ref[...] = _layer_norm(alpha * x + mix, g_ref[...], b_ref[...])


def _merge(x, branches, w_c, dn_w, hg_w, w_branch, w_out, g, b, alpha):
    n = x.shape[0]
    tm = TOKEN_BLOCK
    row = lambda width: pl.BlockSpec((tm, width), lambda i: (i, 0))
    return pl.pallas_call(
        functools.partial(_merge_kernel, alpha=alpha),
        grid=(n // tm,),
        in_specs=[row(D_MODEL)] + [row(WIDTH)] * 6
                 + [_const_spec(a.shape) for a in (w_c, dn_w, hg_w, w_branch, w_out, g, b)],
        out_specs=row(D_MODEL), out_shape=jax.ShapeDtypeStruct((n, D_MODEL), F32),
        compiler_params=_params(1), name="merge",
    )(x, *branches, w_c, dn_w, hg_w, w_branch, w_out, g, b)


def _ffn_kernel(x_ref, p_ref, w1_ref, w2_ref, wg_ref, wp_ref, g_ref, b_ref, o_ref, *, alpha):
    x = x_ref[...]
    xb = _bf(x)
    hid = jnp.square(jnp.maximum(_dot(xb, w1_ref[...]), 0.0))
    ff = _dot(_bf(hid), w2_ref[...])
    ple = jax.nn.sigmoid(_dot(xb, wg_ref[...])) * _dot(_bf(p_ref[...]), wp_ref[...])
    o_ref[...] = _layer_norm(alpha * x + ff + ple, g_ref[...], b_ref[...])


def _ffn(x, p, layer, w1, w2, wg, wp, g, b, alpha):
    n = x.shape[0]
    tm = TOKEN_BLOCK
    row = lambda width: pl.BlockSpec((tm, width), lambda i: (i, 0))
    return pl.pallas_call(
        functools.partial(_ffn_kernel, alpha=alpha),
        grid=(n // tm,),
        in_specs=[row(D_MODEL), pl.BlockSpec((None, tm, D_PLE), lambda i: (layer, i, 0))]
                 + [_const_spec(a.shape) for a in (w1, w2, wg, wp, g, b)],
        out_specs=row(D_MODEL), out_shape=jax.ShapeDtypeStruct((n, D_MODEL), F32),
        compiler_params=_params(1), name="ffn",
    )(x, p, w1, w2, wg, wp, g, b)


def _split_w_in(w):
    c = 0
    offs = {}
    for name, width in (("dqkv", 3 * WIDTH), ("dz", WIDTH), ("dab", 4 * N_HEADS), ("hq_f_f_i", 4 * WIDTH),
                        ("hgate", WIDTH), ("cx", WIDTH), ("cgate", WIDTH), ("gates", 3 * D_MODEL)):
        offs[name] = (c, c + width)
        c += width
    assert c == w.shape[1]
    cols = lambda name: w[:, offs[name][0]:offs[name][1]]
    w_mix = jnp.concatenate([cols("dqkv"), cols("hq_f_f_i"), cols("cx")], axis=1)
    w_narrow = jnp.pad(cols("dab"), ((0, 0), (0, NARROW - 4 * N_HEADS)))
    w_tok = jnp.concatenate([cols("dz"), cols("hgate"), cols("cgate"), cols("gates")], axis=1)
    return _bf(w_mix), _bf(w_narrow), _bf(w_tok)


def _block_diag(w):
    nb, d, _ = w.shape
    return jnp.einsum("kij,km->kimj", w, jnp.eye(nb, dtype=w.dtype)).reshape(nb * d, nb * d)


def _lane_pad(v):
    v = v.reshape(1, -1)
    return jnp.pad(v, ((0, 0), (0, NARROW - v.shape[1])))


def _trunk(x, p, W, depth):
    bsz, t_total, _ = x.shape
    n = bsz * t_total
    alpha = (2.0 * depth) ** 0.25
    x = x.reshape(n, D_MODEL)
    p = p.reshape(depth, n, D_PLE)
    row = lambda v: v.reshape(1, -1)
    for l in range(depth):
        w_mix, w_narrow, w_tok = _split_w_in(W["w_in"][l])
        if l == 0:
            x, h_wide, h_narrow = _in_proj(x, row(W["emb_ln_g"]), row(W["emb_ln_b"]), w_mix, w_narrow, True)
        else:
            h_wide, h_narrow = _in_proj(x, row(W["emb_ln_g"]), row(W["emb_ln_b"]), w_mix, w_narrow, False)
        h_wide = h_wide.reshape(bsz, t_total, -1)
        h_narrow = h_narrow.reshape(bsz, t_total, NARROW)
        alog = _lane_pad(W["dn_A_log"][l])
        dtb = _lane_pad(W["dn_dt_bias"][l])
        branches = []
        for rev in (False, True):
            branches.append(_deltanet(h_wide, h_narrow, W["dn_conv_w"][l], alog, dtb, rev))
        for rev in (False, True):
            branches.append(_hgrn2(h_wide, W["hg_lb_logits"], l, rev))
        for d, rev in enumerate((False, True)):
            w_gate = _bf(jnp.concatenate([_block_diag(W["lru_wa"][l, d]), _block_diag(W["lru_wx"][l, d])], axis=1))
            b_gate = jnp.concatenate([W["lru_ba"][l, d], W["lru_bx"][l, d]]).reshape(1, -1)
            branches.append(_rglru(h_wide, W["lru_conv_w"][l], row(W["lru_conv_b"][l]), w_gate, b_gate,
                                   row(W["lru_lambda"][l, d]), rev))
        branches = [o.reshape(n, WIDTH) for o in branches]
        x = _merge(x, branches, w_tok, row(jnp.tile(W["dn_norm_w"][l], N_HEADS)),
                   row(jnp.tile(W["hg_norm_w"][l], N_HEADS)), _bf(W["w_branch"][l]), _bf(W["w_out"][l]),
                   row(W["ln1_g"][l]), row(W["ln1_b"][l]), alpha)
        x = _ffn(x, p, l, _bf(W["w_mlp1"][l]), _bf(W["w_mlp2"][l]), _bf(W["w_ple_gate"][l]),
                 _bf(W["w_ple_proj"][l]), row(W["ln2_g"][l]), row(W["ln2_b"][l]), alpha)
    return x.reshape(bsz, t_total, D_MODEL)


def kernel(x_prompt, x_sample, p_prompt, p_sample, emb_ln_g, emb_ln_b, w_in, dn_conv_w, dn_A_log, dn_dt_bias,
           dn_norm_w, hg_lb_logits, hg_norm_w, lru_conv_w, lru_conv_b, lru_wa, lru_ba, lru_wx, lru_bx, lru_lambda,
           w_branch, w_out, ln1_g, ln1_b, ln2_g, ln2_b, w_mlp1, w_mlp2, w_ple_gate, w_ple_proj):
    W = dict(emb_ln_g=emb_ln_g, emb_ln_b=emb_ln_b, w_in=w_in, dn_conv_w=dn_conv_w, dn_A_log=dn_A_log,
             dn_dt_bias=dn_dt_bias, dn_norm_w=dn_norm_w, hg_lb_logits=hg_lb_logits, hg_norm_w=hg_norm_w,
             lru_conv_w=lru_conv_w, lru_conv_b=lru_conv_b, lru_wa=lru_wa, lru_ba=lru_ba, lru_wx=lru_wx,
             lru_bx=lru_bx, lru_lambda=lru_lambda, w_branch=w_branch, w_out=w_out, ln1_g=ln1_g, ln1_b=ln1_b,
             ln2_g=ln2_g, ln2_b=ln2_b, w_mlp1=w_mlp1, w_mlp2=w_mlp2, w_ple_gate=w_ple_gate, w_ple_proj=w_ple_proj)
    depth = w_in.shape[0]
    return (_trunk(x_prompt, p_prompt, W, depth), _trunk(x_sample, p_sample, W, depth))
```

```python
import functools

import jax
import jax.numpy as jnp
from jax import lax
from jax.experimental import pallas as pl
from jax.experimental.pallas import tpu as pltpu

F32 = jnp.float32
BF16 = jnp.bfloat16

D_MODEL = 1024
D_PLE = 256
N_HEADS = 4
HEAD_DIM = 128
WIDTH = N_HEADS * HEAD_DIM
CHUNK = 64
SUB = 8
FAST_SUB = 32
FAST_MIN_LOG_DECAY = -60.0
LB_FLOOR = 1e-30
LRU_BLOCKS = 8
LRU_C = 8.0
CONV_WIDTH = 4
D_FF = 4 * D_MODEL
LN_EPS = 1e-5
RMS_EPS = 1e-6
L2_EPS = 1e-6
HALO = 8
NARROW = 128

VMEM_LIMIT_BYTES = 56 * 1024 * 1024
TOKEN_BLOCK = 512
TIME_BLOCK = 256
DN_ROWS = 2

HIGHEST = lax.Precision.HIGHEST


def _dot(a, b, precision=None):
    return jnp.dot(a, b, preferred_element_type=F32, precision=precision)


def _dot_nt(a, b):
    return lax.dot_general(a, b, (((1,), (1,)), ((), ())), preferred_element_type=F32)


def _bf(x):
    return x.astype(BF16)


def _layer_norm(x, g, b):
    mu = jnp.mean(x, -1, keepdims=True)
    xc = x - mu
    var = jnp.mean(xc * xc, -1, keepdims=True)
    return xc * lax.rsqrt(var + LN_EPS) * g + b


def _softplus(x):
    return jnp.maximum(x, 0.0) + jnp.log1p(jnp.exp(-jnp.abs(x)))


def _log_sigmoid(x):
    return jnp.minimum(x, 0.0) - jnp.log1p(jnp.exp(-jnp.abs(x)))


def _silu(x):
    return x * jax.nn.sigmoid(x)


def _const_spec(shape):
    zeros = (0,) * len(shape)
    return pl.BlockSpec(shape, lambda *_: zeros, pipeline_mode=pl.Buffered(1))


def _params(n_axes):
    return pltpu.CompilerParams(dimension_semantics=("arbitrary",) * n_axes,
                                vmem_limit_bytes=VMEM_LIMIT_BYTES)


def _in_proj_kernel(x_ref, g_ref, b_ref, w_ref, wn_ref, *out_refs, apply_ln):
    x = x_ref[...]
    if apply_ln:
        xo_ref, h_ref, hn_ref = out_refs
        x = _layer_norm(x, g_ref[...], b_ref[...])
        xo_ref[...] = x
    else:
        h_ref, hn_ref = out_refs
    xb = _bf(x)
    h_ref[...] = _dot(xb, w_ref[...])
    hn_ref[...] = _dot(xb, wn_ref[...])


def _in_proj(x, g, b, w_wide, w_narrow, apply_ln):
    n = x.shape[0]
    tm = TOKEN_BLOCK
    wide = w_wide.shape[1]
    row = lambda width: pl.BlockSpec((tm, width), lambda i: (i, 0))
    out_shape = [jax.ShapeDtypeStruct((n, wide), F32), jax.ShapeDtypeStruct((n, NARROW), F32)]
    out_specs = [row(wide), row(NARROW)]
    if apply_ln:
        out_shape = [jax.ShapeDtypeStruct((n, D_MODEL), F32)] + out_shape
        out_specs = [row(D_MODEL)] + out_specs
    return pl.pallas_call(
        functools.partial(_in_proj_kernel, apply_ln=apply_ln),
        grid=(n // tm,),
        in_specs=[row(D_MODEL), _const_spec((1, D_MODEL)), _const_spec((1, D_MODEL)),
                  _const_spec(w_wide.shape), _const_spec(w_narrow.shape)],
        out_specs=out_specs, out_shape=out_shape,
        compiler_params=_params(1), name="in_proj",
    )(x, g, b, w_wide, w_narrow)


def _time_index(rev, nt):
    t = pl.program_id(1)
    return (nt - 1 - t) if rev else t


def _fill_halo(xe_ref, main_ref, prev_ref, next_ref, tt, nt, tb, bi=0):
    xe_ref[0:HALO, :] = jnp.where(tt > 0, prev_ref[bi], 0.0)
    xe_ref[HALO:HALO + tb, :] = main_ref[bi]
    xe_ref[HALO + tb:2 * HALO + tb, :] = jnp.where(tt < nt - 1, next_ref[bi], 0.0)


def _conv4(xe_ref, cw, tb):
    acc = cw[0:1, :] * xe_ref[pl.ds(HALO - 1, tb), :]
    for j in range(1, CONV_WIDTH):
        acc = acc + cw[j:j + 1, :] * xe_ref[pl.ds(HALO - 1 + j, tb), :]
    return acc


def _chunk_cumsum(x, rev, window=None):
    tb = x.shape[0]
    rc = lax.broadcasted_iota(jnp.int32, x.shape, 0) % CHUNK
    s = 1
    windowed = None
    while s < CHUNK:
        if s == window:
            windowed = x
        if rev:
            x = x + jnp.where(rc < CHUNK - s, pltpu.roll(x, tb - s, 0), 0.0)
        else:
            x = x + jnp.where(rc >= s, pltpu.roll(x, s, 0), 0.0)
        s *= 2
    return x if window is None else (x, windowed)


def _halo_specs(width, col_block, tb, nt, t_total, rev, nb=1):
    per = tb // HALO
    last = t_total // HALO - 1

    def tidx(t):
        return (nt - 1 - t) if rev else t

    main = pl.BlockSpec((nb, tb, width), lambda b, t: (b, tidx(t), col_block))
    prev = pl.BlockSpec((nb, HALO, width), lambda b, t: (b, jnp.maximum(tidx(t) * per - 1, 0), col_block))
    nxt = pl.BlockSpec((nb, HALO, width), lambda b, t: (b, jnp.minimum((tidx(t) + 1) * per, last), col_block))
    return main, prev, nxt


def _dn_kernel(*refs, rev, tb, nt, nb, prep):
    if prep:
        (main_ref, prev_ref, next_ref, ab_ref, cw_ref, alog_ref, dtb_ref, o_ref, qkv_ref,
         xe_ref, q_s, k_s, v_s, gc_s, beta_s, st_ref) = refs
    else:
        qkv_ref, ab_ref, alog_ref, dtb_ref, o_ref, q_s, k_s, v_s, gc_s, beta_s, st_ref = refs
    tt = _time_index(rev, nt)

    @pl.when(pl.program_id(1) == 0)
    def _():
        st_ref[...] = jnp.zeros_like(st_ref)

    for bi in range(nb):
        if prep:
            _fill_halo(xe_ref, main_ref, prev_ref, next_ref, tt, nt, tb, bi)
            qkv = _silu(_conv4(xe_ref, cw_ref[...], tb))
            for h in range(N_HEADS):
                lo = h * HEAD_DIM
                qh = qkv[:, lo:lo + HEAD_DIM]
                kh = qkv[:, WIDTH + lo:WIDTH + lo + HEAD_DIM]
                q_s[bi, :, lo:lo + HEAD_DIM] = qh * (lax.rsqrt(jnp.sum(qh * qh, -1, keepdims=True) + L2_EPS)
                                                     * (HEAD_DIM ** -0.5))
                k_s[bi, :, lo:lo + HEAD_DIM] = kh * lax.rsqrt(jnp.sum(kh * kh, -1, keepdims=True) + L2_EPS)
            v_s[bi] = qkv[:, 2 * WIDTH:3 * WIDTH]
            qkv_ref[bi, :, 0:WIDTH] = _bf(q_s[bi])
            qkv_ref[bi, :, WIDTH:2 * WIDTH] = _bf(k_s[bi])
            qkv_ref[bi, :, 2 * WIDTH:3 * WIDTH] = _bf(v_s[bi])
        else:
            q_s[bi] = qkv_ref[bi, :, 0:WIDTH].astype(F32)
            k_s[bi] = qkv_ref[bi, :, WIDTH:2 * WIDTH].astype(F32)
            v_s[bi] = qkv_ref[bi, :, 2 * WIDTH:3 * WIDTH].astype(F32)
        ab = ab_ref[bi]
        g = -jnp.exp(alog_ref[...]) * _softplus(ab + dtb_ref[...])
        beta_s[bi] = jax.nn.sigmoid(ab)
        gc_s[bi] = _chunk_cumsum(g, rev)

    ii = lax.broadcasted_iota(jnp.int32, (CHUNK, CHUNK), 0)
    jj = lax.broadcasted_iota(jnp.int32, (CHUNK, CHUNK), 1)
    incl = (ii <= jj) if rev else (ii >= jj)
    strict = (ii < jj) if rev else (ii > jj)
    eye = (ii == jj).astype(F32)
    last = 0 if rev else CHUNK - 1
    nc = tb // CHUNK
    units = [(bi, h) for bi in range(nb) for h in range(N_HEADS)]
    n_u = len(units)
    n_doublings = CHUNK.bit_length() - 2

    def chunk(c, carry):
        cc = (nc - 1 - c) if rev else c
        rows = pl.ds(pl.multiple_of(cc * CHUNK, CHUNK), CHUNK)
        gc = [gc_s[bi, rows, :] for bi in range(nb)]
        gct = [x.T for x in gc]
        bt = [beta_s[bi, rows, :] for bi in range(nb)]
        gcol, bcol, q, k, v, dec, kb, eg, glast = [], [], [], [], [], [], [], [], []
        for bi, h in units:
            lo = h * HEAD_DIM
            lane_g = (N_HEADS if rev else 0) + h
            lane_b = 2 * N_HEADS + lane_g
            gcol.append(gc[bi][:, lane_g:lane_g + 1])
            glast.append(gc[bi][last:last + 1, lane_g:lane_g + 1])
            bcol.append(bt[bi][:, lane_b:lane_b + 1])
            grow = gct[bi][lane_g:lane_g + 1, :]
            q.append(q_s[bi, rows, lo:lo + HEAD_DIM])
            k.append(k_s[bi, rows, lo:lo + HEAD_DIM])
            v.append(v_s[bi, rows, lo:lo + HEAD_DIM])
            dec.append(jnp.where(incl, jnp.exp(jnp.where(incl, gcol[-1] - grow, 0.0)), 0.0))
            kb.append(k[-1] * bcol[-1])
            eg.append(jnp.exp(gcol[-1]))
        qk = [_dot_nt(_bf(jnp.concatenate([q[u], kb[u]], axis=0)), _bf(k[u])) for u in range(n_u)]
        attn = [qk[u][:CHUNK] * dec[u] for u in range(n_u)]
        pw = [-jnp.where(strict, qk[u][CHUNK:] * dec[u], 0.0) for u in range(n_u)]
        tinv = [eye + pw[u] for u in range(n_u)]
        pw = [_bf(x) for x in pw]
        pw = [_bf(_dot(x, x)) for x in pw]
        for step in range(n_doublings):
            for u in range(n_u):
                tinv[u] = tinv[u] + _dot(_bf(tinv[u]), pw[u])
                if step < n_doublings - 1:
                    pw[u] = _bf(_dot(pw[u], pw[u]))
        uw = [_dot(_bf(tinv[u]), _bf(jnp.concatenate([v[u] * bcol[u], kb[u] * eg[u]], axis=1)))
              for u in range(n_u)]
        s_old = [st_ref[bi, h] for bi, h in units]
        wq = [_dot(_bf(jnp.concatenate([uw[u][:, HEAD_DIM:], q[u] * eg[u]], axis=0)), _bf(s_old[u]))
              for u in range(n_u)]
        v_new = [uw[u][:, :HEAD_DIM] - wq[u][:CHUNK] for u in range(n_u)]
        for u, (bi, h) in enumerate(units):
            lo = h * HEAD_DIM
            o_ref[bi, rows, lo:lo + HEAD_DIM] = wq[u][CHUNK:] + _dot(_bf(attn[u]), _bf(v_new[u]))
            kdec = k[u] * jnp.exp(glast[u] - gcol[u])
            st_ref[bi, h] = s_old[u] * jnp.exp(glast[u]) + _dot(_bf(kdec.T), _bf(v_new[u]))
        return carry

    lax.fori_loop(0, nc, chunk, 0)


def _deltanet(h_wide, h_narrow, conv_w, alog_vec, dtb_vec):
    bsz, t_total, _ = h_wide.shape
    tb = TIME_BLOCK
    nb = DN_ROWS
    nt = t_total // tb
    state_scratch = [pltpu.VMEM((nb, tb, WIDTH), F32), pltpu.VMEM((nb, tb, WIDTH), F32),
                     pltpu.VMEM((nb, tb, WIDTH), F32),
                     pltpu.VMEM((nb, tb, NARROW), F32), pltpu.VMEM((nb, tb, NARROW), F32),
                     pltpu.VMEM((nb, N_HEADS, HEAD_DIM, HEAD_DIM), F32)]
    o_shape = jax.ShapeDtypeStruct((bsz, t_total, WIDTH), F32)

    main, prev, nxt = _halo_specs(3 * WIDTH, 0, tb, nt, t_total, False, nb)
    blk = lambda width: pl.BlockSpec((nb, tb, width), lambda b, t: (b, t, 0))
    o_fwd, qkv = pl.pallas_call(
        functools.partial(_dn_kernel, rev=False, tb=tb, nt=nt, nb=nb, prep=True),
        grid=(bsz // nb, nt),
        in_specs=[main, prev, nxt, blk(NARROW),
                  _const_spec(conv_w.shape), _const_spec((1, NARROW)), _const_spec((1, NARROW))],
        out_specs=[blk(WIDTH), blk(3 * WIDTH)],
        out_shape=[o_shape, jax.ShapeDtypeStruct((bsz, t_total, 3 * WIDTH), BF16)],
        scratch_shapes=[pltpu.VMEM((tb + 2 * HALO, 3 * WIDTH), F32)] + state_scratch,
        compiler_params=_params(2), name="deltanet_fwd",
    )(h_wide, h_wide, h_wide, h_narrow, conv_w, alog_vec, dtb_vec)

    rblk = lambda width: pl.BlockSpec((nb, tb, width), lambda b, t: (b, nt - 1 - t, 0))
    o_rev = pl.pallas_call(
        functools.partial(_dn_kernel, rev=True, tb=tb, nt=nt, nb=nb, prep=False),
        grid=(bsz // nb, nt),
        in_specs=[rblk(3 * WIDTH), rblk(NARROW), _const_spec((1, NARROW)), _const_spec((1, NARROW))],
        out_specs=rblk(WIDTH), out_shape=o_shape,
        scratch_shapes=state_scratch,
        compiler_params=_params(2), name="deltanet_rev",
    )(qkv, h_narrow, alog_vec, dtb_vec)
    return o_fwd, o_rev


def _hg_kernel(q_ref, f_ref, i_ref, lg_ref, o_ref, b_s, k_s, st_ref, *, rev, tb, nt, layer):
    @pl.when(pl.program_id(1) == 0)
    def _():
        st_ref[...] = jnp.zeros_like(st_ref)

    lg = lg_ref[...]
    ex = jnp.exp(lg - jnp.max(lg, axis=0, keepdims=True))
    sm = ex / jnp.sum(ex, axis=0, keepdims=True)
    lb = jnp.maximum(jnp.sum(sm[0:layer + 1], axis=0, keepdims=True) - sm[0:1], 0.0)
    lb_floor = jnp.maximum(lb, LB_FLOOR)

    nc = tb // CHUNK

    def gates(c, min_win):
        rows = pl.ds(pl.multiple_of(c * CHUNK, CHUNK), CHUNK)
        fz = f_ref[0, rows, :]
        t = jnp.exp(-jnp.abs(fz))
        pos = fz >= 0.0
        inv = 1.0 / (1.0 + t)
        log_f = jnp.log(jnp.where(pos, 1.0 + lb_floor * t, t + lb_floor)) + jnp.log(inv)
        k_s[rows, :] = (1.0 - lb) * (jnp.where(pos, t, 1.0) * inv)
        b, win = _chunk_cumsum(log_f, rev, FAST_SUB)
        b_s[rows, :] = b
        return jnp.minimum(min_win, jnp.min(win, axis=0, keepdims=True))

    min_win = lax.fori_loop(0, nc, gates, jnp.zeros((1, WIDTH), F32))

    sub_i = lax.broadcasted_iota(jnp.int32, (SUB, HEAD_DIM), 0)
    lane_c = lax.broadcasted_iota(jnp.int32, (SUB, CHUNK), 1)
    fast_i = lax.broadcasted_iota(jnp.int32, (FAST_SUB, CHUNK), 0)
    fast_c = lax.broadcasted_iota(jnp.int32, (FAST_SUB, CHUNK), 1)
    last = 0 if rev else CHUNK - 1

    def pad_rows(x, lo):
        hi = lo + x.shape[0]
        return jnp.concatenate(([jnp.zeros((lo, HEAD_DIM), F32)] if lo else []) + [x]
                               + ([jnp.zeros((CHUNK - hi, HEAD_DIM), F32)] if hi < CHUNK else []), axis=0)

    def boundary(b, a0, sub):
        if rev:
            return b[a0 + sub:a0 + sub + 1] if a0 + sub < CHUNK else 0.0
        return b[a0 - 1:a0] if a0 > 0 else 0.0

    def scores_exact(b, q, k):
        a_rows = []
        for a0 in range(0, CHUNK, SUB):
            b_blk = b[a0:a0 + SUB]
            q_blk = q[a0:a0 + SUB]
            p_lo, p_hi = (a0 + SUB, CHUNK) if rev else (0, a0)
            if p_hi > p_lo:
                bs = boundary(b, a0, SUB)
                k_past = pad_rows(k[p_lo:p_hi] * jnp.exp(bs - b[p_lo:p_hi]), p_lo)
                p = _dot_nt(_bf(q_blk * jnp.exp(b_blk - bs)), _bf(k_past))
            else:
                p = jnp.zeros((SUB, CHUNK), F32)
            for j in range(SUB):
                valid = (sub_i <= j) if rev else (sub_i >= j)
                e = jnp.exp(b_blk - b[a0 + j:a0 + j + 1])
                col = jnp.sum(jnp.where(valid, q_blk * k[a0 + j:a0 + j + 1] * e, 0.0), -1, keepdims=True)
                p = jnp.where(lane_c == a0 + j, col, p)
            a_rows.append(p)
        return jnp.concatenate(a_rows, axis=0)

    def scores_fast(b, q, k):
        a_rows = []
        for a0 in range(0, CHUNK, FAST_SUB):
            bs = boundary(b, a0, FAST_SUB)
            k_lo, k_hi = (a0, CHUNK) if rev else (0, a0 + FAST_SUB)
            k_seen = pad_rows(k[k_lo:k_hi] * jnp.exp(bs - b[k_lo:k_hi]), k_lo)
            p = _dot_nt(_bf(q[a0:a0 + FAST_SUB] * jnp.exp(b[a0:a0 + FAST_SUB] - bs)), _bf(k_seen))
            causal = (fast_c >= fast_i + a0) if rev else (fast_c <= fast_i + a0)
            a_rows.append(jnp.where(causal, p, 0.0))
        return jnp.concatenate(a_rows, axis=0)

    def load(r0, h):
        cols = slice(h * HEAD_DIM, (h + 1) * HEAD_DIM)
        rows = pl.ds(r0, CHUNK)
        return b_s[rows, cols], q_ref[0, rows, cols], k_s[rows, cols], i_ref[0, rows, cols]

    def exact_chunk(c, carry):
        cc = (nc - 1 - c) if rev else c
        r0 = pl.multiple_of(cc * CHUNK, CHUNK)
        for h in range(N_HEADS):
            b, q, k, v = load(r0, h)
            blast = b[last:last + 1]
            st = st_ref[h]
            o = _dot_nt(_bf(q * jnp.exp(b)), _bf(st))
            st_ref[h] = st * jnp.exp(blast) + _dot(_bf(v.T), _bf(k * jnp.exp(blast - b)))
            a = scores_exact(b, q, k)
            o_ref[0, pl.ds(r0, CHUNK), h * HEAD_DIM:(h + 1) * HEAD_DIM] = o + _dot(_bf(a), _bf(v))
        return carry

    def fast_block():
        order = list(range(nc - 1, -1, -1)) if rev else list(range(nc))
        units = [(c, h) for c in order for h in range(N_HEADS)]
        data = {u: load(u[0] * CHUNK, u[1]) for u in units}
        q_dec, d_last, upd = {}, {}, {}
        for u in units:
            b, q, k, v = data[u]
            blast = b[last:last + 1]
            q_dec[u] = _bf(q * jnp.exp(b))
            d_last[u] = jnp.exp(blast)
            upd[u] = _dot(_bf(v.T), _bf(k * jnp.exp(blast - b)))
        intra = {u: scores_fast(*data[u][:3]) for u in units}
        intra = {u: _dot(_bf(intra[u]), _bf(data[u][3])) for u in units}
        for h in range(N_HEADS):
            st = st_ref[h]
            for c in order:
                u = (c, h)
                o_ref[0, c * CHUNK:(c + 1) * CHUNK, h * HEAD_DIM:(h + 1) * HEAD_DIM] = (
                    _dot_nt(q_dec[u], _bf(st)) + intra[u])
                st = st * d_last[u] + upd[u]
            st_ref[h] = st

    bounded = jnp.min(min_win) > FAST_MIN_LOG_DECAY

    @pl.when(bounded)
    def _():
        fast_block()

    @pl.when(jnp.logical_not(bounded))
    def _():
        lax.fori_loop(0, nc, exact_chunk, 0)


def _hgrn2(h_wide, lb_logits, layer, rev):
    bsz, t_total, _ = h_wide.shape
    tb = TIME_BLOCK
    nt = t_total // tb
    tidx = (lambda t: nt - 1 - t) if rev else (lambda t: t)
    base = 3 * WIDTH // WIDTH
    col = lambda c: pl.BlockSpec((1, tb, WIDTH), lambda b, t: (b, tidx(t), c))
    return pl.pallas_call(
        functools.partial(_hg_kernel, rev=rev, tb=tb, nt=nt, layer=layer),
        grid=(bsz, nt),
        in_specs=[col(base), col(base + (2 if rev else 1)), col(base + 3), _const_spec(lb_logits.shape)],
        out_specs=pl.BlockSpec((1, tb, WIDTH), lambda b, t: (b, tidx(t), 0)),
        out_shape=jax.ShapeDtypeStruct((bsz, t_total, WIDTH), F32),
        scratch_shapes=[pltpu.VMEM((tb, WIDTH), F32), pltpu.VMEM((tb, WIDTH), F32),
                        pltpu.VMEM((N_HEADS, HEAD_DIM, HEAD_DIM), F32)],
        compiler_params=_params(2), name="hgrn2_rev" if rev else "hgrn2_fwd",
    )(h_wide, h_wide, h_wide, lb_logits)


def _lru_kernel(main_ref, prev_ref, next_ref, cw_ref, cb_ref, wg_ref, bg_ref, lam_ref, o_ref,
                xe_ref, carry_ref, *, rev, tb, nt):
    tt = _time_index(rev, nt)

    @pl.when(pl.program_id(1) == 0)
    def _():
        carry_ref[...] = jnp.zeros_like(carry_ref)

    _fill_halo(xe_ref, main_ref, prev_ref, next_ref, tt, nt, tb)
    xc = _conv4(xe_ref, cw_ref[...], tb) + cb_ref[...]
    gates = jax.nn.sigmoid(_dot(_bf(xc), wg_ref[...]) + bg_ref[...])
    log_a = LRU_C * gates[:, :WIDTH] * _log_sigmoid(lam_ref[...])
    a = jnp.exp(log_a)
    hval = jnp.sqrt(jnp.maximum(1.0 - jnp.exp(2.0 * log_a), 0.0)) * (gates[:, WIDTH:] * xc)

    row = lax.broadcasted_iota(jnp.int32, (tb, WIDTH), 0)
    s = 1
    while s < tb:
        if rev:
            valid = row < tb - s
            shift = tb - s
        else:
            valid = row >= s
            shift = s
        h_sh = pltpu.roll(hval, shift, 0)
        a_sh = pltpu.roll(a, shift, 0)
        hval = jnp.where(valid, a * h_sh + hval, hval)
        a = jnp.where(valid, a * a_sh, a)
        s *= 2
    hval = hval + a * carry_ref[...]
    last = 0 if rev else tb - 1
    carry_ref[...] = hval[last:last + 1]
    o_ref[0] = hval


def _rglru(h_wide, conv_w, conv_b, w_gate, b_gate, lam, rev):
    bsz, t_total, _ = h_wide.shape
    tb = TIME_BLOCK
    nt = t_total // tb
    col_block = h_wide.shape[2] // WIDTH - 1
    main, prev, nxt = _halo_specs(WIDTH, col_block, tb, nt, t_total, rev)
    tidx = (lambda t: nt - 1 - t) if rev else (lambda t: t)
    return pl.pallas_call(
        functools.partial(_lru_kernel, rev=rev, tb=tb, nt=nt),
        grid=(bsz, nt),
        in_specs=[main, prev, nxt, _const_spec(conv_w.shape), _const_spec(conv_b.shape),
                  _const_spec(w_gate.shape), _const_spec(b_gate.shape), _const_spec(lam.shape)],
        out_specs=pl.BlockSpec((1, tb, WIDTH), lambda b, t: (b, tidx(t), 0)),
        out_shape=jax.ShapeDtypeStruct((bsz, t_total, WIDTH), F32),
        scratch_shapes=[pltpu.VMEM((tb + 2 * HALO, WIDTH), F32), pltpu.VMEM((1, WIDTH), F32)],
        compiler_params=_params(2), name="rglru_rev" if rev else "rglru_fwd",
    )(h_wide, h_wide, h_wide, conv_w, conv_b, w_gate, b_gate, lam)


def _head_rms(o, w, gate):
    parts = []
    for h in range(N_HEADS):
        seg = o[:, h * HEAD_DIM:(h + 1) * HEAD_DIM]
        parts.append(seg * lax.rsqrt(jnp.mean(seg * seg, -1, keepdims=True) + RMS_EPS))
    return jnp.concatenate(parts, axis=1) * w * _silu(gate)


def _merge_kernel(x_ref, dnf_ref, dnb_ref, hgf_ref, hgb_ref, luf_ref, lub_ref, wc_ref, dnw_ref, hgw_ref,
                  wb_ref, wo_ref, g_ref, b_ref, o_ref, *, alpha):
    x = x_ref[...]
    hc = _dot(_bf(x), wc_ref[...])
    o_dn = _head_rms(dnf_ref[...] + dnb_ref[...], dnw_ref[...], hc[:, 0:WIDTH])
    o_hg = _head_rms(hgf_ref[...] + hgb_ref[...], hgw_ref[...], hc[:, WIDTH:2 * WIDTH])
    o_lru = (luf_ref[...] + lub_ref[...]) * jax.nn.gelu(hc[:, 2 * WIDTH:3 * WIDTH])
    mixed = None
    for n, ob in enumerate((o_dn, o_hg, o_lru)):
        lo = 3 * WIDTH + n * D_MODEL
        term = jax.nn.sigmoid(hc[:, lo:lo + D_MODEL]) * _dot(_bf(ob), wb_ref[n])
        mixed = term if mixed is None else mixed + term
    mix = _dot(_bf(mixed), wo_ref[...])
    o_ref[...] = _layer_norm(alpha * x + mix, g_ref[...], b_ref[...])


def _merge(x, branches, w_c, dn_w, hg_w, w_branch, w_out, g, b, alpha):
    n = x.shape[0]
    tm = TOKEN_BLOCK
    row = lambda width: pl.BlockSpec((tm, width), lambda i: (i, 0))
    return pl.pallas_call(
        functools.partial(_merge_kernel, alpha=alpha),
        grid=(n // tm,),
        in_specs=[row(D_MODEL)] + [row(WIDTH)] * 6
                 + [_const_spec(a.shape) for a in (w_c, dn_w, hg_w, w_branch, w_out, g, b)],
        out_specs=row(D_MODEL), out_shape=jax.ShapeDtypeStruct((n, D_MODEL), F32),
        compiler_params=_params(1), name="merge",
    )(x, *branches, w_c, dn_w, hg_w, w_branch, w_out, g, b)


def _ffn_kernel(x_ref, p_ref, w1_ref, w2_ref, wg_ref, wp_ref, g_ref, b_ref, o_ref, *, alpha):
    x = x_ref[...]
    xb = _bf(x)
    hid = jnp.square(jnp.maximum(_dot(xb, w1_ref[...]), 0.0))
    ff = _dot(_bf(hid), w2_ref[...])
    ple = jax.nn.sigmoid(_dot(xb, wg_ref[...])) * _dot(_bf(p_ref[...]), wp_ref[...])
    o_ref[...] = _layer_norm(alpha * x + ff + ple, g_ref[...], b_ref[...])


def _ffn(x, p, layer, w1, w2, wg, wp, g, b, alpha):
    n = x.shape[0]
    tm = TOKEN_BLOCK
    row = lambda width: pl.BlockSpec((tm, width), lambda i: (i, 0))
    return pl.pallas_call(
        functools.partial(_ffn_kernel, alpha=alpha),
        grid=(n // tm,),
        in_specs=[row(D_MODEL), pl.BlockSpec((None, tm, D_PLE), lambda i: (layer, i, 0))]
                 + [_const_spec(a.shape) for a in (w1, w2, wg, wp, g, b)],
        out_specs=row(D_MODEL), out_shape=jax.ShapeDtypeStruct((n, D_MODEL), F32),
        compiler_params=_params(1), name="ffn",
    )(x, p, w1, w2, wg, wp, g, b)


def _split_w_in(w):
    c = 0
    offs = {}
    for name, width in (("dqkv", 3 * WIDTH), ("dz", WIDTH), ("dab", 4 * N_HEADS), ("hq_f_f_i", 4 * WIDTH),
                        ("hgate", WIDTH), ("cx", WIDTH), ("cgate", WIDTH), ("gates", 3 * D_MODEL)):
        offs[name] = (c, c + width)
        c += width
    assert c == w.shape[1]
    cols = lambda name: w[:, offs[name][0]:offs[name][1]]
    w_mix = jnp.concatenate([cols("dqkv"), cols("hq_f_f_i"), cols("cx")], axis=1)
    w_narrow = jnp.pad(cols("dab"), ((0, 0), (0, NARROW - 4 * N_HEADS)))
    w_tok = jnp.concatenate([cols("dz"), cols("hgate"), cols("cgate"), cols("gates")], axis=1)
    return _bf(w_mix), _bf(w_narrow), _bf(w_tok)


def _block_diag(w):
    nb, d, _ = w.shape
    return jnp.einsum("kij,km->kimj", w, jnp.eye(nb, dtype=w.dtype)).reshape(nb * d, nb * d)


def _lane_pad(v):
    v = v.reshape(1, -1)
    return jnp.pad(v, ((0, 0), (0, NARROW - v.shape[1])))


def _trunk(x, p, W, depth):
    bsz, t_total, _ = x.shape
    n = bsz * t_total
    alpha = (2.0 * depth) ** 0.25
    x = x.reshape(n, D_MODEL)
    p = p.reshape(depth, n, D_PLE)
    row = lambda v: v.reshape(1, -1)
    for l in range(depth):
        w_mix, w_narrow, w_tok = _split_w_in(W["w_in"][l])
        if l == 0:
            x, h_wide, h_narrow = _in_proj(x, row(W["emb_ln_g"]), row(W["emb_ln_b"]), w_mix, w_narrow, True)
        else:
            h_wide, h_narrow = _in_proj(x, row(W["emb_ln_g"]), row(W["emb_ln_b"]), w_mix, w_narrow, False)
        h_wide = h_wide.reshape(bsz, t_total, -1)
        h_narrow = h_narrow.reshape(bsz, t_total, NARROW)
        alog = _lane_pad(W["dn_A_log"][l])
        dtb = _lane_pad(W["dn_dt_bias"][l])
        branches = list(_deltanet(h_wide, h_narrow, W["dn_conv_w"][l], alog, dtb))
        for rev in (False, True):
            branches.append(_hgrn2(h_wide, W["hg_lb_logits"], l, rev))
        for d, rev in enumerate((False, True)):
            w_gate = _bf(jnp.concatenate([_block_diag(W["lru_wa"][l, d]), _block_diag(W["lru_wx"][l, d])], axis=1))
            b_gate = jnp.concatenate([W["lru_ba"][l, d], W["lru_bx"][l, d]]).reshape(1, -1)
            branches.append(_rglru(h_wide, W["lru_conv_w"][l], row(W["lru_conv_b"][l]), w_gate, b_gate,
                                   row(W["lru_lambda"][l, d]), rev))
        branches = [o.reshape(n, WIDTH) for o in branches]
        x = _merge(x, branches, w_tok, row(jnp.tile(W["dn_norm_w"][l], N_HEADS)),
                   row(jnp.tile(W["hg_norm_w"][l], N_HEADS)), _bf(W["w_branch"][l]), _bf(W["w_out"][l]),
                   row(W["ln1_g"][l]), row(W["ln1_b"][l]), alpha)
        x = _ffn(x, p, l, _bf(W["w_mlp1"][l]), _bf(W["w_mlp2"][l]), _bf(W["w_ple_gate"][l]),
                 _bf(W["w_ple_proj"][l]), row(W["ln2_g"][l]), row(W["ln2_b"][l]), alpha)
    return x.reshape(bsz, t_total, D_MODEL)


def kernel(x_prompt, x_sample, p_prompt, p_sample, emb_ln_g, emb_ln_b, w_in, dn_conv_w, dn_A_log, dn_dt_bias,
           dn_norm_w, hg_lb_logits, hg_norm_w, lru_conv_w, lru_conv_b, lru_wa, lru_ba, lru_wx, lru_bx, lru_lambda,
           w_branch, w_out, ln1_g, ln1_b, ln2_g, ln2_b, w_mlp1, w_mlp2, w_ple_gate, w_ple_proj):
    W = dict(emb_ln_g=emb_ln_g, emb_ln_b=emb_ln_b, w_in=w_in, dn_conv_w=dn_conv_w, dn_A_log=dn_A_log,
             dn_dt_bias=dn_dt_bias, dn_norm_w=dn_norm_w, hg_lb_logits=hg_lb_logits, hg_norm_w=hg_norm_w,
             lru_conv_w=lru_conv_w, lru_conv_b=lru_conv_b, lru_wa=lru_wa, lru_ba=lru_ba, lru_wx=lru_wx,
             lru_bx=lru_bx, lru_lambda=lru_lambda, w_branch=w_branch, w_out=w_out, ln1_g=ln1_g, ln1_b=ln1_b,
             ln2_g=ln2_g, ln2_b=ln2_b, w_mlp1=w_mlp1, w_mlp2=w_mlp2, w_ple_gate=w_ple_gate, w_ple_proj=w_ple_proj)
    depth = w_in.shape[0]
    return (_trunk(x_prompt, p_prompt, W, depth), _trunk(x_sample, p_sample, W, depth))
```

```python
import functools

import jax
import jax.numpy as jnp
from jax import lax
from jax.experimental import pallas as pl
from jax.experimental.pallas import tpu as pltpu

F32 = jnp.float32
BF16 = jnp.bfloat16

D_MODEL = 1024
D_PLE = 256
N_HEADS = 4
HEAD_DIM = 128
WIDTH = N_HEADS * HEAD_DIM
CHUNK = 64
SUB = 8
FAST_SUB = 32
FAST_MIN_LOG_DECAY = -60.0
LB_FLOOR = 1e-30
LRU_BLOCKS = 8
LRU_C = 8.0
CONV_WIDTH = 4
D_FF = 4 * D_MODEL
LN_EPS = 1e-5
RMS_EPS = 1e-6
L2_EPS = 1e-6
SUBLANES = 8
HALO = 8
NARROW = 128

VMEM_LIMIT_BYTES = 56 * 1024 * 1024
TOKEN_BLOCK = 512
TIME_BLOCK = 256
DN_ROWS = (1, 2, 4)


def _dot(a, b):
    return jnp.dot(a, b, preferred_element_type=F32)


def _dot_nt(a, b):
    return lax.dot_general(a, b, (((1,), (1,)), ((), ())), preferred_element_type=F32)


def _bf(x):
    return x.astype(BF16)


def _layer_norm(x, g, b):
    mu = jnp.mean(x, -1, keepdims=True)
    xc = x - mu
    var = jnp.mean(xc * xc, -1, keepdims=True)
    return xc * lax.rsqrt(var + LN_EPS) * g + b


def _softplus(x):
    return jnp.maximum(x, 0.0) + jnp.log1p(jnp.exp(-jnp.abs(x)))


def _log_sigmoid(x):
    return jnp.minimum(x, 0.0) - jnp.log1p(jnp.exp(-jnp.abs(x)))


def _silu(x):
    return x * jax.nn.sigmoid(x)


def _const_spec(shape):
    zeros = (0,) * len(shape)
    return pl.BlockSpec(shape, lambda *_: zeros, pipeline_mode=pl.Buffered(1))


def _params(n_axes):
    return pltpu.CompilerParams(dimension_semantics=("arbitrary",) * n_axes,
                                vmem_limit_bytes=VMEM_LIMIT_BYTES)


def _in_proj_kernel(x_ref, xp_ref, xn_ref, g_ref, b_ref, wc_ref, wh_ref, wn_ref, cw_ref, cb_ref, *refs,
                    apply_ln, blocks_per_seq):
    if apply_ln:
        xo_ref, hg_ref, hn_ref, qkv_ref, xc_ref, xe_ref = refs
    else:
        hg_ref, hn_ref, qkv_ref, xc_ref, xe_ref = refs
    tm = x_ref.shape[0]
    pos = pl.program_id(0) % blocks_per_seq
    x = x_ref[...]
    x_halo = jnp.concatenate([xp_ref[...], xn_ref[...]], axis=0)
    if apply_ln:
        x = _layer_norm(x, g_ref[...], b_ref[...])
        x_halo = _layer_norm(x_halo, g_ref[...], b_ref[...])
        xo_ref[...] = x
    xb = _bf(x)
    hg_ref[...] = _dot(xb, wh_ref[...])
    hn_ref[...] = _dot(xb, wn_ref[...])
    h_halo = _dot(_bf(x_halo), wc_ref[...])
    xe_ref[0:HALO, :] = jnp.where(pos > 0, h_halo[0:HALO], 0.0)
    xe_ref[HALO:HALO + tm, :] = _dot(xb, wc_ref[...])
    xe_ref[HALO + tm:2 * HALO + tm, :] = jnp.where(pos < blocks_per_seq - 1, h_halo[HALO:2 * HALO], 0.0)
    conv = _conv4(xe_ref, cw_ref[...], tm)
    xc_ref[...] = conv[:, 3 * WIDTH:] + cb_ref[...]
    qkv = _silu(conv[:, :3 * WIDTH])
    for h in range(N_HEADS):
        lo = h * HEAD_DIM
        qh = qkv[:, lo:lo + HEAD_DIM]
        kh = qkv[:, WIDTH + lo:WIDTH + lo + HEAD_DIM]
        qkv_ref[:, lo:lo + HEAD_DIM] = _bf(qh * (lax.rsqrt(jnp.sum(qh * qh, -1, keepdims=True) + L2_EPS)
                                                 * (HEAD_DIM ** -0.5)))
        qkv_ref[:, WIDTH + lo:WIDTH + lo + HEAD_DIM] = _bf(
            kh * lax.rsqrt(jnp.sum(kh * kh, -1, keepdims=True) + L2_EPS))
    qkv_ref[:, 2 * WIDTH:3 * WIDTH] = _bf(qkv[:, 2 * WIDTH:3 * WIDTH])


def _in_proj(x, g, b, w_conv, w_hg, w_narrow, conv_w, conv_b, apply_ln, t_total):
    n = x.shape[0]
    tm = TOKEN_BLOCK
    per = tm // HALO
    last = n // HALO - 1
    row = lambda width: pl.BlockSpec((tm, width), lambda i: (i, 0))
    widths = [(w_hg.shape[1], F32), (NARROW, F32), (3 * WIDTH, BF16), (WIDTH, F32)]
    if apply_ln:
        widths = [(D_MODEL, F32)] + widths
    return pl.pallas_call(
        functools.partial(_in_proj_kernel, apply_ln=apply_ln, blocks_per_seq=t_total // tm),
        grid=(n // tm,),
        in_specs=[row(D_MODEL),
                  pl.BlockSpec((HALO, D_MODEL), lambda i: (jnp.maximum(i * per - 1, 0), 0)),
                  pl.BlockSpec((HALO, D_MODEL), lambda i: (jnp.minimum((i + 1) * per, last), 0))]
                 + [_const_spec(a.shape) for a in (g, b, w_conv, w_hg, w_narrow, conv_w, conv_b)],
        out_specs=[row(w) for w, _ in widths],
        out_shape=[jax.ShapeDtypeStruct((n, w), dt) for w, dt in widths],
        scratch_shapes=[pltpu.VMEM((tm + 2 * HALO, w_conv.shape[1]), F32)],
        compiler_params=_params(1), name="in_proj",
    )(x, x, x, g, b, w_conv, w_hg, w_narrow, conv_w, conv_b)


def _conv4(xe_ref, cw, tb):
    acc = cw[0:1, :] * xe_ref[pl.ds(HALO - 1, tb), :]
    for j in range(1, CONV_WIDTH):
        acc = acc + cw[j:j + 1, :] * xe_ref[pl.ds(HALO - 1 + j, tb), :]
    return acc


def _chunk_cumsum(x, rev, window=None):
    tb = x.shape[0]
    rc = lax.broadcasted_iota(jnp.int32, x.shape, 0) % CHUNK
    s = 1
    windowed = None
    while s < CHUNK:
        if s == window:
            windowed = x
        if rev:
            x = x + jnp.where(rc < CHUNK - s, pltpu.roll(x, tb - s, 0), 0.0)
        else:
            x = x + jnp.where(rc >= s, pltpu.roll(x, s, 0), 0.0)
        s *= 2
    return x if window is None else (x, windowed)


def _dn_kernel(qkv_ref, ab_ref, alog_ref, dtb_ref, o_ref, q_s, k_s, v_s, gc_s, beta_s, st_ref, *, rev, tb, nt, nb):
    @pl.when(pl.program_id(1) == 0)
    def _():
        st_ref[...] = jnp.zeros_like(st_ref)

    for bi in range(nb):
        q_s[bi] = qkv_ref[bi, :, 0:WIDTH].astype(F32)
        k_s[bi] = qkv_ref[bi, :, WIDTH:2 * WIDTH].astype(F32)
        v_s[bi] = qkv_ref[bi, :, 2 * WIDTH:3 * WIDTH].astype(F32)
        ab = ab_ref[bi]
        g = -jnp.exp(alog_ref[...]) * _softplus(ab + dtb_ref[...])
        beta_s[bi] = jax.nn.sigmoid(ab)
        gc_s[bi] = _chunk_cumsum(g, rev)

    ii = lax.broadcasted_iota(jnp.int32, (CHUNK, CHUNK), 0)
    jj = lax.broadcasted_iota(jnp.int32, (CHUNK, CHUNK), 1)
    incl = (ii <= jj) if rev else (ii >= jj)
    strict = (ii < jj) if rev else (ii > jj)
    eye = (ii == jj).astype(F32)
    last = 0 if rev else CHUNK - 1
    nc = tb // CHUNK
    units = [(bi, h) for bi in range(nb) for h in range(N_HEADS)]
    n_u = len(units)
    n_doublings = CHUNK.bit_length() - 2

    def chunk(c, carry):
        cc = (nc - 1 - c) if rev else c
        rows = pl.ds(pl.multiple_of(cc * CHUNK, CHUNK), CHUNK)
        gc = [gc_s[bi, rows, :] for bi in range(nb)]
        gct = [x.T for x in gc]
        bt = [beta_s[bi, rows, :] for bi in range(nb)]
        gcol, bcol, q, k, v, dec, kb, eg, glast = [], [], [], [], [], [], [], [], []
        for bi, h in units:
            lo = h * HEAD_DIM
            lane_g = (N_HEADS if rev else 0) + h
            lane_b = 2 * N_HEADS + lane_g
            gcol.append(gc[bi][:, lane_g:lane_g + 1])
            glast.append(gc[bi][last:last + 1, lane_g:lane_g + 1])
            bcol.append(bt[bi][:, lane_b:lane_b + 1])
            grow = gct[bi][lane_g:lane_g + 1, :]
            q.append(q_s[bi, rows, lo:lo + HEAD_DIM])
            k.append(k_s[bi, rows, lo:lo + HEAD_DIM])
            v.append(v_s[bi, rows, lo:lo + HEAD_DIM])
            dec.append(jnp.where(incl, jnp.exp(jnp.where(incl, gcol[-1] - grow, 0.0)), 0.0))
            kb.append(k[-1] * bcol[-1])
            eg.append(jnp.exp(gcol[-1]))
        qk = [_dot_nt(_bf(jnp.concatenate([q[u], kb[u]], axis=0)), _bf(k[u])) for u in range(n_u)]
        attn = [qk[u][:CHUNK] * dec[u] for u in range(n_u)]
        pw = [-jnp.where(strict, qk[u][CHUNK:] * dec[u], 0.0) for u in range(n_u)]
        tinv = [eye + pw[u] for u in range(n_u)]
        pw = [_bf(x) for x in pw]
        pw = [_bf(_dot(x, x)) for x in pw]
        for step in range(n_doublings):
            for u in range(n_u):
                tinv[u] = tinv[u] + _dot(_bf(tinv[u]), pw[u])
                if step < n_doublings - 1:
                    pw[u] = _bf(_dot(pw[u], pw[u]))
        uw = [_dot(_bf(tinv[u]), _bf(jnp.concatenate([v[u] * bcol[u], kb[u] * eg[u]], axis=1)))
              for u in range(n_u)]
        s_old = [st_ref[bi, h] for bi, h in units]
        wq = [_dot(_bf(jnp.concatenate([uw[u][:, HEAD_DIM:], q[u] * eg[u]], axis=0)), _bf(s_old[u]))
              for u in range(n_u)]
        v_new = [uw[u][:, :HEAD_DIM] - wq[u][:CHUNK] for u in range(n_u)]
        for u, (bi, h) in enumerate(units):
            lo = h * HEAD_DIM
            o_ref[bi, rows, lo:lo + HEAD_DIM] = wq[u][CHUNK:] + _dot(_bf(attn[u]), _bf(v_new[u]))
            kdec = k[u] * jnp.exp(glast[u] - gcol[u])
            st_ref[bi, h] = s_old[u] * jnp.exp(glast[u]) + _dot(_bf(kdec.T), _bf(v_new[u]))
        return carry

    lax.fori_loop(0, nc, chunk, 0)


def _deltanet(qkv, h_narrow, alog_vec, dtb_vec, rev):
    bsz, t_total, _ = qkv.shape
    tb = TIME_BLOCK
    nb = max(r for r in DN_ROWS if bsz % r == 0)
    nt = t_total // tb
    tidx = (lambda t: nt - 1 - t) if rev else (lambda t: t)
    blk = lambda width: pl.BlockSpec((nb, tb, width), lambda b, t: (b, tidx(t), 0))
    return pl.pallas_call(
        functools.partial(_dn_kernel, rev=rev, tb=tb, nt=nt, nb=nb),
        grid=(bsz // nb, nt),
        in_specs=[blk(3 * WIDTH), blk(NARROW), _const_spec((1, NARROW)), _const_spec((1, NARROW))],
        out_specs=blk(WIDTH), out_shape=jax.ShapeDtypeStruct((bsz, t_total, WIDTH), F32),
        scratch_shapes=[pltpu.VMEM((nb, tb, WIDTH), F32), pltpu.VMEM((nb, tb, WIDTH), F32),
                        pltpu.VMEM((nb, tb, WIDTH), F32),
                        pltpu.VMEM((nb, tb, NARROW), F32), pltpu.VMEM((nb, tb, NARROW), F32),
                        pltpu.VMEM((nb, N_HEADS, HEAD_DIM, HEAD_DIM), F32)],
        compiler_params=_params(2), name="deltanet_rev" if rev else "deltanet_fwd",
    )(qkv, h_narrow, alog_vec, dtb_vec)


def _hg_kernel(q_ref, f_ref, i_ref, lg_ref, o_ref, b_s, k_s, st_ref, *, rev, tb, nt, layer):
    @pl.when(pl.program_id(1) == 0)
    def _():
        st_ref[...] = jnp.zeros_like(st_ref)

    lg = lg_ref[...]
    ex = jnp.exp(lg - jnp.max(lg, axis=0, keepdims=True))
    sm = ex / jnp.sum(ex, axis=0, keepdims=True)
    lb = jnp.maximum(jnp.sum(sm[0:layer + 1], axis=0, keepdims=True) - sm[0:1], 0.0)
    lb_floor = jnp.maximum(lb, LB_FLOOR)

    nc = tb // CHUNK
    def gates(c, carry):
        rows = pl.ds(pl.multiple_of(c * CHUNK, CHUNK), CHUNK)
        fz = f_ref[0, rows, :]
        t = jnp.exp(-jnp.abs(fz))
        pos = fz >= 0.0
        inv = 1.0 / (1.0 + t)
        log_f = jnp.log(jnp.where(pos, 1.0 + lb_floor * t, t + lb_floor)) + jnp.log(inv)
        k_s[rows, :] = (1.0 - lb) * (jnp.where(pos, t, 1.0) * inv)
        b, win = _chunk_cumsum(log_f, rev, FAST_SUB)
        b_s[rows, :] = b
        return jnp.minimum(carry, jnp.min(win, axis=0, keepdims=True))

    min_win = lax.fori_loop(0, nc, gates, jnp.zeros((1, WIDTH), F32))

    sub_i = lax.broadcasted_iota(jnp.int32, (SUB, HEAD_DIM), 0)
    lane_c = lax.broadcasted_iota(jnp.int32, (SUB, CHUNK), 1)
    fast_i = lax.broadcasted_iota(jnp.int32, (FAST_SUB, CHUNK), 0)
    fast_c = lax.broadcasted_iota(jnp.int32, (FAST_SUB, CHUNK), 1)
    last = 0 if rev else CHUNK - 1

    def pad_rows(x, lo):
        hi = lo + x.shape[0]
        return jnp.concatenate(([jnp.zeros((lo, HEAD_DIM), F32)] if lo else []) + [x]
                               + ([jnp.zeros((CHUNK - hi, HEAD_DIM), F32)] if hi < CHUNK else []), axis=0)

    def boundary(b, a0, sub):
        if rev:
            return b[a0 + sub:a0 + sub + 1] if a0 + sub < CHUNK else 0.0
        return b[a0 - 1:a0] if a0 > 0 else 0.0

    def scores_exact(b, q, k):
        a_rows = []
        for a0 in range(0, CHUNK, SUB):
            b_blk = b[a0:a0 + SUB]
            q_blk = q[a0:a0 + SUB]
            p_lo, p_hi = (a0 + SUB, CHUNK) if rev else (0, a0)
            if p_hi > p_lo:
                bs = boundary(b, a0, SUB)
                k_past = pad_rows(k[p_lo:p_hi] * jnp.exp(bs - b[p_lo:p_hi]), p_lo)
                p = _dot_nt(_bf(q_blk * jnp.exp(b_blk - bs)), _bf(k_past))
            else:
                p = jnp.zeros((SUB, CHUNK), F32)
            for j in range(SUB):
                valid = (sub_i <= j) if rev else (sub_i >= j)
                e = jnp.exp(b_blk - b[a0 + j:a0 + j + 1])
                col = jnp.sum(jnp.where(valid, q_blk * k[a0 + j:a0 + j + 1] * e, 0.0), -1, keepdims=True)
                p = jnp.where(lane_c == a0 + j, col, p)
            a_rows.append(p)
        return jnp.concatenate(a_rows, axis=0)

    def scores_fast(b, q, k):
        a_rows = []
        for a0 in range(0, CHUNK, FAST_SUB):
            bs = boundary(b, a0, FAST_SUB)
            k_lo, k_hi = (a0, CHUNK) if rev else (0, a0 + FAST_SUB)
            k_seen = pad_rows(k[k_lo:k_hi] * jnp.exp(bs - b[k_lo:k_hi]), k_lo)
            p = _dot_nt(_bf(q[a0:a0 + FAST_SUB] * jnp.exp(b[a0:a0 + FAST_SUB] - bs)), _bf(k_seen))
            causal = (fast_c >= fast_i + a0) if rev else (fast_c <= fast_i + a0)
            a_rows.append(jnp.where(causal, p, 0.0))
        return jnp.concatenate(a_rows, axis=0)

    def load(r0, h):
        cols = slice(h * HEAD_DIM, (h + 1) * HEAD_DIM)
        rows = pl.ds(r0, CHUNK)
        return b_s[rows, cols], q_ref[0, rows, cols], k_s[rows, cols], i_ref[0, rows, cols]

    def exact_chunk(c, carry):
        cc = (nc - 1 - c) if rev else c
        r0 = pl.multiple_of(cc * CHUNK, CHUNK)
        for h in range(N_HEADS):
            b, q, k, v = load(r0, h)
            blast = b[last:last + 1]
            st = st_ref[h]
            o = _dot_nt(_bf(q * jnp.exp(b)), _bf(st))
            st_ref[h] = st * jnp.exp(blast) + _dot(_bf(v.T), _bf(k * jnp.exp(blast - b)))
            a = scores_exact(b, q, k)
            o_ref[0, pl.ds(r0, CHUNK), h * HEAD_DIM:(h + 1) * HEAD_DIM] = o + _dot(_bf(a), _bf(v))
        return carry

    def fast_block():
        order = list(range(nc - 1, -1, -1)) if rev else list(range(nc))
        units = [(c, h) for c in order for h in range(N_HEADS)]
        data = {u: load(u[0] * CHUNK, u[1]) for u in units}
        q_dec, d_last, upd = {}, {}, {}
        for u in units:
            b, q, k, v = data[u]
            blast = b[last:last + 1]
            q_dec[u] = _bf(q * jnp.exp(b))
            d_last[u] = jnp.exp(blast)
            upd[u] = _dot(_bf(v.T), _bf(k * jnp.exp(blast - b)))
        intra = {u: scores_fast(*data[u][:3]) for u in units}
        intra = {u: _dot(_bf(intra[u]), _bf(data[u][3])) for u in units}
        for h in range(N_HEADS):
            st = st_ref[h]
            for c in order:
                u = (c, h)
                o_ref[0, c * CHUNK:(c + 1) * CHUNK, h * HEAD_DIM:(h + 1) * HEAD_DIM] = (
                    _dot_nt(q_dec[u], _bf(st)) + intra[u])
                st = st * d_last[u] + upd[u]
            st_ref[h] = st

    bounded = jnp.min(min_win) > FAST_MIN_LOG_DECAY

    @pl.when(bounded)
    def _():
        fast_block()

    @pl.when(jnp.logical_not(bounded))
    def _():
        lax.fori_loop(0, nc, exact_chunk, 0)


def _hgrn2(h_wide, lb_logits, layer, rev):
    bsz, t_total, _ = h_wide.shape
    tb = TIME_BLOCK
    nt = t_total // tb
    tidx = (lambda t: nt - 1 - t) if rev else (lambda t: t)
    col = lambda c: pl.BlockSpec((1, tb, WIDTH), lambda b, t: (b, tidx(t), c))
    return pl.pallas_call(
        functools.partial(_hg_kernel, rev=rev, tb=tb, nt=nt, layer=layer),
        grid=(bsz, nt),
        in_specs=[col(0), col(2 if rev else 1), col(3), _const_spec(lb_logits.shape)],
        out_specs=pl.BlockSpec((1, tb, WIDTH), lambda b, t: (b, tidx(t), 0)),
        out_shape=jax.ShapeDtypeStruct((bsz, t_total, WIDTH), F32),
        scratch_shapes=[pltpu.VMEM((tb, WIDTH), F32), pltpu.VMEM((tb, WIDTH), F32),
                        pltpu.VMEM((N_HEADS, HEAD_DIM, HEAD_DIM), F32)],
        compiler_params=_params(2), name="hgrn2_rev" if rev else "hgrn2_fwd",
    )(h_wide, h_wide, h_wide, lb_logits)


def _linear_scan(a, h, axis, rev):
    n = a.shape[axis]
    idx = lax.broadcasted_iota(jnp.int32, a.shape, axis)
    s = 1
    while s < n:
        valid = (idx < n - s) if rev else (idx >= s)
        shift = (n - s) if rev else s
        h = jnp.where(valid, a * pltpu.roll(h, shift, axis) + h, h)
        a = jnp.where(valid, a * pltpu.roll(a, shift, axis), a)
        s *= 2
    return a, h


def _lru_kernel(xc_ref, wg_ref, bg_ref, lam_ref, o_ref, carry_ref, a_s, h_s, *, rev, tb):
    @pl.when(pl.program_id(1) == 0)
    def _():
        carry_ref[...] = jnp.zeros_like(carry_ref)

    xc = xc_ref[0]
    gates = jax.nn.sigmoid(_dot(_bf(xc), wg_ref[...]) + bg_ref[...])
    log_a = LRU_C * gates[:, :WIDTH] * _log_sigmoid(lam_ref[...])
    a = jnp.exp(log_a)
    hval = jnp.sqrt(jnp.maximum(1.0 - a * a, 0.0)) * (gates[:, WIDTH:] * xc)

    ng = tb // SUBLANES
    a, hval = _linear_scan(a.reshape(ng, SUBLANES, WIDTH), hval.reshape(ng, SUBLANES, WIDTH), 1, rev)
    a = a.reshape(tb, WIDTH)
    hval = hval.reshape(tb, WIDTH)
    n_tiles = WIDTH // HEAD_DIM
    for j in range(n_tiles):
        a_s[j] = a[:, j * HEAD_DIM:(j + 1) * HEAD_DIM]
        h_s[j] = hval[:, j * HEAD_DIM:(j + 1) * HEAD_DIM]
    edge = 0 if rev else SUBLANES - 1
    totals = lambda ref: jnp.concatenate([ref[j, pl.ds(edge, ng, stride=SUBLANES), :] for j in range(n_tiles)],
                                         axis=1)
    a_grp, h_grp = _linear_scan(totals(a_s), totals(h_s), 0, rev)
    carry = carry_ref[...]
    state = h_grp + a_grp * carry
    g = lax.broadcasted_iota(jnp.int32, (ng, WIDTH), 0)
    if rev:
        carry_in = jnp.where(g < ng - 1, pltpu.roll(state, ng - 1, 0), carry)
        carry_ref[...] = state[0:1]
    else:
        carry_in = jnp.where(g >= 1, pltpu.roll(state, 1, 0), carry)
        carry_ref[...] = state[ng - 1:ng]
    for gi in range(ng):
        rows = slice(gi * SUBLANES, (gi + 1) * SUBLANES)
        for j in range(n_tiles):
            lanes = slice(j * HEAD_DIM, (j + 1) * HEAD_DIM)
            o_ref[0, rows, lanes] = h_s[j, rows, :] + a_s[j, rows, :] * carry_in[gi:gi + 1, lanes]


def _rglru(xc, w_gate, b_gate, lam, rev):
    bsz, t_total, _ = xc.shape
    tb = TIME_BLOCK
    nt = t_total // tb
    tidx = (lambda t: nt - 1 - t) if rev else (lambda t: t)
    blk = pl.BlockSpec((1, tb, WIDTH), lambda b, t: (b, tidx(t), 0))
    return pl.pallas_call(
        functools.partial(_lru_kernel, rev=rev, tb=tb),
        grid=(bsz, nt),
        in_specs=[blk, _const_spec(w_gate.shape), _const_spec(b_gate.shape), _const_spec(lam.shape)],
        out_specs=blk, out_shape=jax.ShapeDtypeStruct((bsz, t_total, WIDTH), F32),
        scratch_shapes=[pltpu.VMEM((1, WIDTH), F32),
                        pltpu.VMEM((WIDTH // HEAD_DIM, tb, HEAD_DIM), F32),
                        pltpu.VMEM((WIDTH // HEAD_DIM, tb, HEAD_DIM), F32)],
        compiler_params=_params(2), name="rglru_rev" if rev else "rglru_fwd",
    )(xc, w_gate, b_gate, lam)


def _head_rms(o, w, gate):
    parts = []
    for h in range(N_HEADS):
        seg = o[:, h * HEAD_DIM:(h + 1) * HEAD_DIM]
        parts.append(seg * lax.rsqrt(jnp.mean(seg * seg, -1, keepdims=True) + RMS_EPS))
    return jnp.concatenate(parts, axis=1) * w * _silu(gate)


def _merge_kernel(x_ref, dnf_ref, dnb_ref, hgf_ref, hgb_ref, luf_ref, lub_ref, wc_ref, dnw_ref, hgw_ref,
                  wb_ref, wo_ref, g_ref, b_ref, o_ref, *, alpha):
    x = x_ref[...]
    hc = _dot(_bf(x), wc_ref[...])
    o_dn = _head_rms(dnf_ref[...] + dnb_ref[...], dnw_ref[...], hc[:, 0:WIDTH])
    o_hg = _head_rms(hgf_ref[...] + hgb_ref[...], hgw_ref[...], hc[:, WIDTH:2 * WIDTH])
    o_lru = (luf_ref[...] + lub_ref[...]) * jax.nn.gelu(hc[:, 2 * WIDTH:3 * WIDTH])
    mixed = None
    for n, ob in enumerate((o_dn, o_hg, o_lru)):
        lo = 3 * WIDTH + n * D_MODEL
        term = jax.nn.sigmoid(hc[:, lo:lo + D_MODEL]) * _dot(_bf(ob), wb_ref[n])
        mixed = term if mixed is None else mixed + term
    mix = _dot(_bf(mixed), wo_ref[...])
    o_ref[...] = _layer_norm(alpha * x + mix, g_ref[...], b_ref[...])


def _merge(x, branches, w_c, dn_w, hg_w, w_branch, w_out, g, b, alpha):
    n = x.shape[0]
    tm = TOKEN_BLOCK
    row = lambda width: pl.BlockSpec((tm, width), lambda i: (i, 0))
    return pl.pallas_call(
        functools.partial(_merge_kernel, alpha=alpha),
        grid=(n // tm,),
        in_specs=[row(D_MODEL)] + [row(WIDTH)] * 6
                 + [_const_spec(a.shape) for a in (w_c, dn_w, hg_w, w_branch, w_out, g, b)],
        out_specs=row(D_MODEL), out_shape=jax.ShapeDtypeStruct((n, D_MODEL), F32),
        compiler_params=_params(1), name="merge",
    )(x, *branches, w_c, dn_w, hg_w, w_branch, w_out, g, b)


def _ffn_kernel(x_ref, p_ref, w1_ref, w2_ref, wg_ref, wp_ref, g_ref, b_ref, o_ref, *, alpha):
    x = x_ref[...]
    xb = _bf(x)
    hid = jnp.square(jnp.maximum(_dot(xb, w1_ref[...]), 0.0))
    ff = _dot(_bf(hid), w2_ref[...])
    ple = jax.nn.sigmoid(_dot(xb, wg_ref[...])) * _dot(_bf(p_ref[...]), wp_ref[...])
    o_ref[...] = _layer_norm(alpha * x + ff + ple, g_ref[...], b_ref[...])


def _ffn(x, p, layer, w1, w2, wg, wp, g, b, alpha):
    n = x.shape[0]
    tm = TOKEN_BLOCK
    row = lambda width: pl.BlockSpec((tm, width), lambda i: (i, 0))
    return pl.pallas_call(
        functools.partial(_ffn_kernel, alpha=alpha),
        grid=(n // tm,),
        in_specs=[row(D_MODEL), pl.BlockSpec((None, tm, D_PLE), lambda i: (layer, i, 0))]
                 + [_const_spec(a.shape) for a in (w1, w2, wg, wp, g, b)],
        out_specs=row(D_MODEL), out_shape=jax.ShapeDtypeStruct((n, D_MODEL), F32),
        compiler_params=_params(1), name="ffn",
    )(x, p, w1, w2, wg, wp, g, b)


def _split_w_in(w):
    c = 0
    offs = {}
    for name, width in (("dqkv", 3 * WIDTH), ("dz", WIDTH), ("dab", 4 * N_HEADS), ("hq_f_f_i", 4 * WIDTH),
                        ("hgate", WIDTH), ("cx", WIDTH), ("cgate", WIDTH), ("gates", 3 * D_MODEL)):
        offs[name] = (c, c + width)
        c += width
    assert c == w.shape[1]
    cols = lambda name: w[:, offs[name][0]:offs[name][1]]
    w_conv = jnp.concatenate([cols("dqkv"), cols("cx")], axis=1)
    w_narrow = jnp.pad(cols("dab"), ((0, 0), (0, NARROW - 4 * N_HEADS)))
    w_tok = jnp.concatenate([cols("dz"), cols("hgate"), cols("cgate"), cols("gates")], axis=1)
    return _bf(w_conv), _bf(cols("hq_f_f_i")), _bf(w_narrow), _bf(w_tok)


def _block_diag(w):
    nb, d, _ = w.shape
    return jnp.einsum("kij,km->kimj", w, jnp.eye(nb, dtype=w.dtype)).reshape(nb * d, nb * d)


def _lane_pad(v):
    v = v.reshape(1, -1)
    return jnp.pad(v, ((0, 0), (0, NARROW - v.shape[1])))


def _trunk(x, p, W, depth):
    bsz, t_total, _ = x.shape
    n = bsz * t_total
    alpha = (2.0 * depth) ** 0.25
    x = x.reshape(n, D_MODEL)
    p = p.reshape(depth, n, D_PLE)
    row = lambda v: v.reshape(1, -1)
    for l in range(depth):
        w_conv, w_hg, w_narrow, w_tok = _split_w_in(W["w_in"][l])
        conv_w = jnp.concatenate([W["dn_conv_w"][l], W["lru_conv_w"][l]], axis=1)
        outs = _in_proj(x, row(W["emb_ln_g"]), row(W["emb_ln_b"]), w_conv, w_hg, w_narrow, conv_w,
                        row(W["lru_conv_b"][l]), l == 0, t_total)
        if l == 0:
            x = outs[0]
        h_hg, h_narrow, qkv, xc = [o.reshape(bsz, t_total, -1) for o in outs[-4:]]
        alog = _lane_pad(W["dn_A_log"][l])
        dtb = _lane_pad(W["dn_dt_bias"][l])
        branches = [_deltanet(qkv, h_narrow, alog, dtb, rev) for rev in (False, True)]
        for rev in (False, True):
            branches.append(_hgrn2(h_hg, W["hg_lb_logits"], l, rev))
        for d, rev in enumerate((False, True)):
            w_gate = _bf(jnp.concatenate([_block_diag(W["lru_wa"][l, d]), _block_diag(W["lru_wx"][l, d])], axis=1))
            b_gate = jnp.concatenate([W["lru_ba"][l, d], W["lru_bx"][l, d]]).reshape(1, -1)
            branches.append(_rglru(xc, w_gate, b_gate, row(W["lru_lambda"][l, d]), rev))
        branches = [o.reshape(n, WIDTH) for o in branches]
        x = _merge(x, branches, w_tok, row(jnp.tile(W["dn_norm_w"][l], N_HEADS)),
                   row(jnp.tile(W["hg_norm_w"][l], N_HEADS)), _bf(W["w_branch"][l]), _bf(W["w_out"][l]),
                   row(W["ln1_g"][l]), row(W["ln1_b"][l]), alpha)
        x = _ffn(x, p, l, _bf(W["w_mlp1"][l]), _bf(W["w_mlp2"][l]), _bf(W["w_ple_gate"][l]),
                 _bf(W["w_ple_proj"][l]), row(W["ln2_g"][l]), row(W["ln2_b"][l]), alpha)
    return x.reshape(bsz, t_total, D_MODEL)


def kernel(x_prompt, x_sample, p_prompt, p_sample, emb_ln_g, emb_ln_b, w_in, dn_conv_w, dn_A_log, dn_dt_bias,
           dn_norm_w, hg_lb_logits, hg_norm_w, lru_conv_w, lru_conv_b, lru_wa, lru_ba, lru_wx, lru_bx, lru_lambda,
           w_branch, w_out, ln1_g, ln1_b, ln2_g, ln2_b, w_mlp1, w_mlp2, w_ple_gate, w_ple_proj):
    W = dict(emb_ln_g=emb_ln_g, emb_ln_b=emb_ln_b, w_in=w_in, dn_conv_w=dn_conv_w, dn_A_log=dn_A_log,
             dn_dt_bias=dn_dt_bias, dn_norm_w=dn_norm_w, hg_lb_logits=hg_lb_logits, hg_norm_w=hg_norm_w,
             lru_conv_w=lru_conv_w, lru_conv_b=lru_conv_b, lru_wa=lru_wa, lru_ba=lru_ba, lru_wx=lru_wx,
             lru_bx=lru_bx, lru_lambda=lru_lambda, w_branch=w_branch, w_out=w_out, ln1_g=ln1_g, ln1_b=ln1_b,
             ln2_g=ln2_g, ln2_b=ln2_b, w_mlp1=w_mlp1, w_mlp2=w_mlp2, w_ple_gate=w_ple_gate, w_ple_proj=w_ple_proj)
    depth = w_in.shape[0]
    return (_trunk(x_prompt, p_prompt, W, depth), _trunk(x_sample, p_sample, W, depth))
```

```python
import functools

import jax
import jax.numpy as jnp
from jax import lax
from jax.experimental import pallas as pl
from jax.experimental.pallas import tpu as pltpu

F32 = jnp.float32
BF16 = jnp.bfloat16

D_MODEL = 1024
D_PLE = 256
N_HEADS = 4
HEAD_DIM = 128
WIDTH = N_HEADS * HEAD_DIM
CHUNK = 64
SUB = 8
FAST_SUB = 32
FAST_MIN_LOG_DECAY = -60.0
LB_FLOOR = 1e-30
LRU_BLOCKS = 8
LRU_C = 8.0
CONV_WIDTH = 4
D_FF = 4 * D_MODEL
LN_EPS = 1e-5
RMS_EPS = 1e-6
L2_EPS = 1e-6
SUBLANES = 8
HALO = 8
NARROW = 128

VMEM_LIMIT_BYTES = 56 * 1024 * 1024
TOKEN_BLOCK = 512
IN_PROJ_SUBBLOCKS = 2
TIME_BLOCK = 256
DN_ROWS = 2


def _dot(a, b):
    return jnp.dot(a, b, preferred_element_type=F32)


def _dot_nt(a, b):
    return lax.dot_general(a, b, (((1,), (1,)), ((), ())), preferred_element_type=F32)


def _bf(x):
    return x.astype(BF16)


def _layer_norm(x, g, b):
    mu = jnp.mean(x, -1, keepdims=True)
    xc = x - mu
    var = jnp.mean(xc * xc, -1, keepdims=True)
    return xc * lax.rsqrt(var + LN_EPS) * g + b


def _softplus(x):
    return jnp.maximum(x, 0.0) + jnp.log1p(jnp.exp(-jnp.abs(x)))


def _log_sigmoid(x):
    return jnp.minimum(x, 0.0) - jnp.log1p(jnp.exp(-jnp.abs(x)))


def _silu(x):
    return x * jax.nn.sigmoid(x)


def _const_spec(shape):
    zeros = (0,) * len(shape)
    return pl.BlockSpec(shape, lambda *_: zeros, pipeline_mode=pl.Buffered(1))


def _params(n_axes):
    return pltpu.CompilerParams(dimension_semantics=("arbitrary",) * n_axes,
                                vmem_limit_bytes=VMEM_LIMIT_BYTES)


def _in_proj_kernel(x_ref, xp_ref, xn_ref, g_ref, b_ref, wc_ref, wh_ref, wn_ref, cw_ref, cb_ref, *refs,
                    apply_ln, blocks_per_seq):
    n_sub = IN_PROJ_SUBBLOCKS
    if apply_ln:
        xo_ref, hg_ref, hn_ref, qkv_ref, xc_ref = refs[:5]
    else:
        hg_ref, hn_ref, qkv_ref, xc_ref = refs[:4]
    xe_refs = refs[-n_sub:]
    tm = x_ref.shape[0]
    sub = tm // n_sub
    pos = pl.program_id(0) % blocks_per_seq

    def project(j):
        r0 = j * sub
        rows = slice(r0, r0 + sub)
        before = x_ref[r0 - HALO:r0, :] if j > 0 else xp_ref[...]
        after = x_ref[r0 + sub:r0 + sub + HALO, :] if j < n_sub - 1 else xn_ref[...]
        x_ext = jnp.concatenate([before, x_ref[rows, :], after], axis=0)
        if apply_ln:
            x_ext = _layer_norm(x_ext, g_ref[...], b_ref[...])
            xo_ref[rows, :] = x_ext[HALO:HALO + sub]
        xb = _bf(x_ext[HALO:HALO + sub])
        hg_ref[rows, :] = _dot(xb, wh_ref[...])
        hn_ref[rows, :] = _dot(xb, wn_ref[...])
        h_ext = _dot(_bf(x_ext), wc_ref[...])
        in_seq_before = True if j > 0 else pos > 0
        in_seq_after = True if j < n_sub - 1 else pos < blocks_per_seq - 1
        xe_refs[j][0:HALO, :] = jnp.where(in_seq_before, h_ext[0:HALO], 0.0)
        xe_refs[j][HALO:HALO + sub, :] = h_ext[HALO:HALO + sub]
        xe_refs[j][HALO + sub:, :] = jnp.where(in_seq_after, h_ext[HALO + sub:], 0.0)

    def finish(j):
        rows = slice(j * sub, (j + 1) * sub)
        conv = _conv4(xe_refs[j], cw_ref[...], sub, 0)
        xc_ref[rows, :] = conv[:, 3 * WIDTH:] + cb_ref[...]
        qkv = _silu(conv[:, :3 * WIDTH])
        for h in range(N_HEADS):
            lo = h * HEAD_DIM
            qh = qkv[:, lo:lo + HEAD_DIM]
            kh = qkv[:, WIDTH + lo:WIDTH + lo + HEAD_DIM]
            qkv_ref[rows, lo:lo + HEAD_DIM] = _bf(qh * (lax.rsqrt(jnp.sum(qh * qh, -1, keepdims=True) + L2_EPS)
                                                        * (HEAD_DIM ** -0.5)))
            qkv_ref[rows, WIDTH + lo:WIDTH + lo + HEAD_DIM] = _bf(
                kh * lax.rsqrt(jnp.sum(kh * kh, -1, keepdims=True) + L2_EPS))
        qkv_ref[rows, 2 * WIDTH:3 * WIDTH] = _bf(qkv[:, 2 * WIDTH:3 * WIDTH])

    project(0)
    for j in range(1, IN_PROJ_SUBBLOCKS):
        project(j)
        finish(j - 1)
    finish(IN_PROJ_SUBBLOCKS - 1)


def _in_proj(x, g, b, w_conv, w_hg, w_narrow, conv_w, conv_b, apply_ln, t_total):
    n = x.shape[0]
    tm = TOKEN_BLOCK
    per = tm // HALO
    last = n // HALO - 1
    row = lambda width: pl.BlockSpec((tm, width), lambda i: (i, 0))
    widths = [(w_hg.shape[1], F32), (NARROW, F32), (3 * WIDTH, BF16), (WIDTH, F32)]
    if apply_ln:
        widths = [(D_MODEL, F32)] + widths
    out_specs = [row(w) for w, _ in widths]
    out_shape = [jax.ShapeDtypeStruct((n, w), dt) for w, dt in widths]
    return pl.pallas_call(
        functools.partial(_in_proj_kernel, apply_ln=apply_ln, blocks_per_seq=t_total // tm),
        grid=(n // tm,),
        in_specs=[row(D_MODEL),
                  pl.BlockSpec((HALO, D_MODEL), lambda i: (jnp.maximum(i * per - 1, 0), 0)),
                  pl.BlockSpec((HALO, D_MODEL), lambda i: (jnp.minimum((i + 1) * per, last), 0))]
                 + [_const_spec(a.shape) for a in (g, b, w_conv, w_hg, w_narrow, conv_w, conv_b)],
        out_specs=out_specs, out_shape=out_shape,
        scratch_shapes=[pltpu.VMEM((tm // IN_PROJ_SUBBLOCKS + 2 * HALO, w_conv.shape[1]), F32)] * IN_PROJ_SUBBLOCKS,
        compiler_params=_params(1), name="in_proj",
    )(x, x, x, g, b, w_conv, w_hg, w_narrow, conv_w, conv_b)


def _conv4(xe_ref, cw, rows, r0):
    acc = cw[0:1, :] * xe_ref[pl.ds(r0 + HALO - 1, rows), :]
    for j in range(1, CONV_WIDTH):
        acc = acc + cw[j:j + 1, :] * xe_ref[pl.ds(r0 + HALO - 1 + j, rows), :]
    return acc


def _chunk_cumsum(x, rev, window=None):
    tb = x.shape[0]
    rc = lax.broadcasted_iota(jnp.int32, x.shape, 0) % CHUNK
    s = 1
    windowed = None
    while s < CHUNK:
        if s == window:
            windowed = x
        if rev:
            x = x + jnp.where(rc < CHUNK - s, pltpu.roll(x, tb - s, 0), 0.0)
        else:
            x = x + jnp.where(rc >= s, pltpu.roll(x, s, 0), 0.0)
        s *= 2
    return x if window is None else (x, windowed)


def _dn_kernel(qkv_f_ref, ab_f_ref, qkv_r_ref, ab_r_ref, alog_ref, dtb_ref, of_ref, or_ref,
               q_s, k_s, v_s, gc_s, beta_s, st_ref, *, tb, nb):
    @pl.when(pl.program_id(1) == 0)
    def _():
        st_ref[...] = jnp.zeros_like(st_ref)

    streams = [(rev, bi) for rev in (False, True) for bi in range(nb)]
    for s, (rev, bi) in enumerate(streams):
        qkv_ref, ab_ref = (qkv_r_ref, ab_r_ref) if rev else (qkv_f_ref, ab_f_ref)
        q_s[s] = qkv_ref[bi, :, 0:WIDTH].astype(F32)
        k_s[s] = qkv_ref[bi, :, WIDTH:2 * WIDTH].astype(F32)
        v_s[s] = qkv_ref[bi, :, 2 * WIDTH:3 * WIDTH].astype(F32)
        ab = ab_ref[bi]
        g = -jnp.exp(alog_ref[...]) * _softplus(ab + dtb_ref[...])
        beta_s[s] = jax.nn.sigmoid(ab)
        gc_s[s] = _chunk_cumsum(g, rev)

    ii = lax.broadcasted_iota(jnp.int32, (CHUNK, CHUNK), 0)
    jj = lax.broadcasted_iota(jnp.int32, (CHUNK, CHUNK), 1)
    incl = {False: ii >= jj, True: ii <= jj}
    strict = {False: ii > jj, True: ii < jj}
    eye = (ii == jj).astype(F32)
    nc = tb // CHUNK
    units = [(s, h) for s in range(len(streams)) for h in range(N_HEADS)]
    n_u = len(units)
    n_doublings = CHUNK.bit_length() - 2

    def chunk(c, carry):
        rows = {False: pl.ds(pl.multiple_of(c * CHUNK, CHUNK), CHUNK),
                True: pl.ds(pl.multiple_of((nc - 1 - c) * CHUNK, CHUNK), CHUNK)}
        gc = [gc_s[s, rows[rev], :] for s, (rev, _) in enumerate(streams)]
        gct = [x.T for x in gc]
        bt = [beta_s[s, rows[rev], :] for s, (rev, _) in enumerate(streams)]
        gcol, bcol, q, k, v, dec, kb, eg, glast = [], [], [], [], [], [], [], [], []
        for s, h in units:
            rev = streams[s][0]
            lo = h * HEAD_DIM
            lane_g = (N_HEADS if rev else 0) + h
            lane_b = 2 * N_HEADS + lane_g
            last = 0 if rev else CHUNK - 1
            gcol.append(gc[s][:, lane_g:lane_g + 1])
            glast.append(gc[s][last:last + 1, lane_g:lane_g + 1])
            bcol.append(bt[s][:, lane_b:lane_b + 1])
            grow = gct[s][lane_g:lane_g + 1, :]
            q.append(q_s[s, rows[rev], lo:lo + HEAD_DIM])
            k.append(k_s[s, rows[rev], lo:lo + HEAD_DIM])
            v.append(v_s[s, rows[rev], lo:lo + HEAD_DIM])
            dec.append(jnp.where(incl[rev], jnp.exp(jnp.where(incl[rev], gcol[-1] - grow, 0.0)), 0.0))
            kb.append(k[-1] * bcol[-1])
            eg.append(jnp.exp(gcol[-1]))
        qk = [_dot_nt(_bf(jnp.concatenate([q[u], kb[u]], axis=0)), _bf(k[u])) for u in range(n_u)]
        attn = [qk[u][:CHUNK] * dec[u] for u in range(n_u)]
        pw = [-jnp.where(strict[streams[s][0]], qk[u][CHUNK:] * dec[u], 0.0) for u, (s, _) in enumerate(units)]
        tinv = [eye + pw[u] for u in range(n_u)]
        pw = [_bf(x) for x in pw]
        pw = [_bf(_dot(x, x)) for x in pw]
        for step in range(n_doublings):
            for u in range(n_u):
                tinv[u] = tinv[u] + _dot(_bf(tinv[u]), pw[u])
                if step < n_doublings - 1:
                    pw[u] = _bf(_dot(pw[u], pw[u]))
        uw = [_dot(_bf(tinv[u]), _bf(jnp.concatenate([v[u] * bcol[u], kb[u] * eg[u]], axis=1)))
              for u in range(n_u)]
        s_old = [st_ref[s, h] for s, h in units]
        wq = [_dot(_bf(jnp.concatenate([uw[u][:, HEAD_DIM:], q[u] * eg[u]], axis=0)), _bf(s_old[u]))
              for u in range(n_u)]
        v_new = [uw[u][:, :HEAD_DIM] - wq[u][:CHUNK] for u in range(n_u)]
        for u, (s, h) in enumerate(units):
            rev, bi = streams[s]
            lo = h * HEAD_DIM
            o_ref = or_ref if rev else of_ref
            o_ref[bi, rows[rev], lo:lo + HEAD_DIM] = wq[u][CHUNK:] + _dot(_bf(attn[u]), _bf(v_new[u]))
            kdec = k[u] * jnp.exp(glast[u] - gcol[u])
            st_ref[s, h] = s_old[u] * jnp.exp(glast[u]) + _dot(_bf(kdec.T), _bf(v_new[u]))
        return carry

    lax.fori_loop(0, nc, chunk, 0)


def _deltanet(qkv, h_narrow, alog_vec, dtb_vec):
    bsz, t_total, _ = qkv.shape
    tb = TIME_BLOCK
    nb = DN_ROWS
    nt = t_total // tb
    fwd = lambda width: pl.BlockSpec((nb, tb, width), lambda b, t: (b, t, 0))
    rev = lambda width: pl.BlockSpec((nb, tb, width), lambda b, t: (b, nt - 1 - t, 0))
    o_shape = jax.ShapeDtypeStruct((bsz, t_total, WIDTH), F32)
    n_streams = 2 * nb
    return pl.pallas_call(
        functools.partial(_dn_kernel, tb=tb, nb=nb),
        grid=(bsz // nb, nt),
        in_specs=[fwd(3 * WIDTH), fwd(NARROW), rev(3 * WIDTH), rev(NARROW),
                  _const_spec((1, NARROW)), _const_spec((1, NARROW))],
        out_specs=[fwd(WIDTH), rev(WIDTH)], out_shape=[o_shape, o_shape],
        scratch_shapes=[pltpu.VMEM((n_streams, tb, WIDTH), F32), pltpu.VMEM((n_streams, tb, WIDTH), F32),
                        pltpu.VMEM((n_streams, tb, WIDTH), F32),
                        pltpu.VMEM((n_streams, tb, NARROW), F32), pltpu.VMEM((n_streams, tb, NARROW), F32),
                        pltpu.VMEM((n_streams, N_HEADS, HEAD_DIM, HEAD_DIM), F32)],
        compiler_params=_params(2), name="deltanet",
    )(qkv, h_narrow, qkv, h_narrow, alog_vec, dtb_vec)


def _hg_kernel(q_ref, f_ref, i_ref, lg_ref, o_ref, b_s, k_s, st_ref, *, rev, tb, nt, layer):
    @pl.when(pl.program_id(1) == 0)
    def _():
        st_ref[...] = jnp.zeros_like(st_ref)

    lg = lg_ref[...]
    ex = jnp.exp(lg - jnp.max(lg, axis=0, keepdims=True))
    sm = ex / jnp.sum(ex, axis=0, keepdims=True)
    lb = jnp.maximum(jnp.sum(sm[0:layer + 1], axis=0, keepdims=True) - sm[0:1], 0.0)
    lb_floor = jnp.maximum(lb, LB_FLOOR)

    nc = tb // CHUNK
    def gates(c, carry):
        rows = pl.ds(pl.multiple_of(c * CHUNK, CHUNK), CHUNK)
        fz = f_ref[0, rows, :]
        t = jnp.exp(-jnp.abs(fz))
        pos = fz >= 0.0
        inv = 1.0 / (1.0 + t)
        log_f = jnp.log(jnp.where(pos, 1.0 + lb_floor * t, t + lb_floor)) + jnp.log(inv)
        k_s[rows, :] = (1.0 - lb) * (jnp.where(pos, t, 1.0) * inv)
        b, win = _chunk_cumsum(log_f, rev, FAST_SUB)
        b_s[rows, :] = b
        return jnp.minimum(carry, jnp.min(win, axis=0, keepdims=True))

    min_win = lax.fori_loop(0, nc, gates, jnp.zeros((1, WIDTH), F32))

    sub_i = lax.broadcasted_iota(jnp.int32, (SUB, HEAD_DIM), 0)
    lane_c = lax.broadcasted_iota(jnp.int32, (SUB, CHUNK), 1)
    fast_i = lax.broadcasted_iota(jnp.int32, (FAST_SUB, CHUNK), 0)
    fast_c = lax.broadcasted_iota(jnp.int32, (FAST_SUB, CHUNK), 1)
    last = 0 if rev else CHUNK - 1

    def pad_rows(x, lo):
        hi = lo + x.shape[0]
        return jnp.concatenate(([jnp.zeros((lo, HEAD_DIM), F32)] if lo else []) + [x]
                               + ([jnp.zeros((CHUNK - hi, HEAD_DIM), F32)] if hi < CHUNK else []), axis=0)

    def boundary(b, a0, sub):
        if rev:
            return b[a0 + sub:a0 + sub + 1] if a0 + sub < CHUNK else 0.0
        return b[a0 - 1:a0] if a0 > 0 else 0.0

    def scores_exact(b, q, k):
        a_rows = []
        for a0 in range(0, CHUNK, SUB):
            b_blk = b[a0:a0 + SUB]
            q_blk = q[a0:a0 + SUB]
            p_lo, p_hi = (a0 + SUB, CHUNK) if rev else (0, a0)
            if p_hi > p_lo:
                bs = boundary(b, a0, SUB)
                k_past = pad_rows(k[p_lo:p_hi] * jnp.exp(bs - b[p_lo:p_hi]), p_lo)
                p = _dot_nt(_bf(q_blk * jnp.exp(b_blk - bs)), _bf(k_past))
            else:
                p = jnp.zeros((SUB, CHUNK), F32)
            for j in range(SUB):
                valid = (sub_i <= j) if rev else (sub_i >= j)
                e = jnp.exp(b_blk - b[a0 + j:a0 + j + 1])
                col = jnp.sum(jnp.where(valid, q_blk * k[a0 + j:a0 + j + 1] * e, 0.0), -1, keepdims=True)
                p = jnp.where(lane_c == a0 + j, col, p)
            a_rows.append(p)
        return jnp.concatenate(a_rows, axis=0)

    def scores_fast(b, q, k):
        a_rows = []
        for a0 in range(0, CHUNK, FAST_SUB):
            bs = boundary(b, a0, FAST_SUB)
            k_lo, k_hi = (a0, CHUNK) if rev else (0, a0 + FAST_SUB)
            k_seen = pad_rows(k[k_lo:k_hi] * jnp.exp(bs - b[k_lo:k_hi]), k_lo)
            p = _dot_nt(_bf(q[a0:a0 + FAST_SUB] * jnp.exp(b[a0:a0 + FAST_SUB] - bs)), _bf(k_seen))
            causal = (fast_c >= fast_i + a0) if rev else (fast_c <= fast_i + a0)
            a_rows.append(jnp.where(causal, p, 0.0))
        return jnp.concatenate(a_rows, axis=0)

    def load(r0, h):
        cols = slice(h * HEAD_DIM, (h + 1) * HEAD_DIM)
        rows = pl.ds(r0, CHUNK)
        return b_s[rows, cols], q_ref[0, rows, cols], k_s[rows, cols], i_ref[0, rows, cols]

    def exact_chunk(c, carry):
        cc = (nc - 1 - c) if rev else c
        r0 = pl.multiple_of(cc * CHUNK, CHUNK)
        for h in range(N_HEADS):
            b, q, k, v = load(r0, h)
            blast = b[last:last + 1]
            st = st_ref[h]
            o = _dot_nt(_bf(q * jnp.exp(b)), _bf(st))
            st_ref[h] = st * jnp.exp(blast) + _dot(_bf(v.T), _bf(k * jnp.exp(blast - b)))
            a = scores_exact(b, q, k)
            o_ref[0, pl.ds(r0, CHUNK), h * HEAD_DIM:(h + 1) * HEAD_DIM] = o + _dot(_bf(a), _bf(v))
        return carry

    def fast_block():
        order = list(range(nc - 1, -1, -1)) if rev else list(range(nc))
        units = [(c, h) for c in order for h in range(N_HEADS)]
        data = {u: load(u[0] * CHUNK, u[1]) for u in units}
        q_dec, d_last, upd = {}, {}, {}
        for u in units:
            b, q, k, v = data[u]
            blast = b[last:last + 1]
            q_dec[u] = _bf(q * jnp.exp(b))
            d_last[u] = jnp.exp(blast)
            upd[u] = _dot(_bf(v.T), _bf(k * jnp.exp(blast - b)))
        inter = {}
        for h in range(N_HEADS):
            st = st_ref[h]
            for c in order:
                u = (c, h)
                inter[u] = _dot_nt(q_dec[u], _bf(st))
                st = st * d_last[u] + upd[u]
            st_ref[h] = st
        intra = {u: scores_fast(*data[u][:3]) for u in units}
        for c, h in units:
            o_ref[0, c * CHUNK:(c + 1) * CHUNK, h * HEAD_DIM:(h + 1) * HEAD_DIM] = (
                inter[c, h] + _dot(_bf(intra[c, h]), _bf(data[c, h][3])))

    bounded = jnp.min(min_win) > FAST_MIN_LOG_DECAY

    @pl.when(bounded)
    def _():
        fast_block()

    @pl.when(jnp.logical_not(bounded))
    def _():
        lax.fori_loop(0, nc, exact_chunk, 0)


def _hgrn2(h_wide, lb_logits, layer, rev):
    bsz, t_total, _ = h_wide.shape
    tb = TIME_BLOCK
    nt = t_total // tb
    tidx = (lambda t: nt - 1 - t) if rev else (lambda t: t)
    col = lambda c: pl.BlockSpec((1, tb, WIDTH), lambda b, t: (b, tidx(t), c))
    return pl.pallas_call(
        functools.partial(_hg_kernel, rev=rev, tb=tb, nt=nt, layer=layer),
        grid=(bsz, nt),
        in_specs=[col(0), col(2 if rev else 1), col(3), _const_spec(lb_logits.shape)],
        out_specs=pl.BlockSpec((1, tb, WIDTH), lambda b, t: (b, tidx(t), 0)),
        out_shape=jax.ShapeDtypeStruct((bsz, t_total, WIDTH), F32),
        scratch_shapes=[pltpu.VMEM((tb, WIDTH), F32), pltpu.VMEM((tb, WIDTH), F32),
                        pltpu.VMEM((N_HEADS, HEAD_DIM, HEAD_DIM), F32)],
        compiler_params=_params(2), name="hgrn2_rev" if rev else "hgrn2_fwd",
    )(h_wide, h_wide, h_wide, lb_logits)


def _linear_scan(a, h, axis, rev):
    n = a.shape[axis]
    idx = lax.broadcasted_iota(jnp.int32, a.shape, axis)
    s = 1
    while s < n:
        valid = (idx < n - s) if rev else (idx >= s)
        shift = (n - s) if rev else s
        h = jnp.where(valid, a * pltpu.roll(h, shift, axis) + h, h)
        a = jnp.where(valid, a * pltpu.roll(a, shift, axis), a)
        s *= 2
    return a, h


def _lru_kernel(xc_ref, wg_ref, bg_ref, lam_ref, o_ref, carry_ref, a_s, h_s, *, rev, tb):
    @pl.when(pl.program_id(1) == 0)
    def _():
        carry_ref[...] = jnp.zeros_like(carry_ref)

    xc = xc_ref[0]
    gates = jax.nn.sigmoid(_dot(_bf(xc), wg_ref[...]) + bg_ref[...])
    log_a = LRU_C * gates[:, :WIDTH] * _log_sigmoid(lam_ref[...])
    a = jnp.exp(log_a)
    hval = jnp.sqrt(jnp.maximum(1.0 - a * a, 0.0)) * (gates[:, WIDTH:] * xc)

    ng = tb // SUBLANES
    a, hval = _linear_scan(a.reshape(ng, SUBLANES, WIDTH), hval.reshape(ng, SUBLANES, WIDTH), 1, rev)
    a = a.reshape(tb, WIDTH)
    hval = hval.reshape(tb, WIDTH)
    n_tiles = WIDTH // HEAD_DIM
    for j in range(n_tiles):
        a_s[j] = a[:, j * HEAD_DIM:(j + 1) * HEAD_DIM]
        h_s[j] = hval[:, j * HEAD_DIM:(j + 1) * HEAD_DIM]
    edge = 0 if rev else SUBLANES - 1
    totals = lambda ref: jnp.concatenate([ref[j, pl.ds(edge, ng, stride=SUBLANES), :] for j in range(n_tiles)],
                                         axis=1)
    a_grp, h_grp = _linear_scan(totals(a_s), totals(h_s), 0, rev)
    carry = carry_ref[...]
    state = h_grp + a_grp * carry
    g = lax.broadcasted_iota(jnp.int32, (ng, WIDTH), 0)
    if rev:
        carry_in = jnp.where(g < ng - 1, pltpu.roll(state, ng - 1, 0), carry)
        carry_ref[...] = state[0:1]
    else:
        carry_in = jnp.where(g >= 1, pltpu.roll(state, 1, 0), carry)
        carry_ref[...] = state[ng - 1:ng]
    for gi in range(ng):
        rows = slice(gi * SUBLANES, (gi + 1) * SUBLANES)
        for j in range(n_tiles):
            lanes = slice(j * HEAD_DIM, (j + 1) * HEAD_DIM)
            o_ref[0, rows, lanes] = h_s[j, rows, :] + a_s[j, rows, :] * carry_in[gi:gi + 1, lanes]


def _rglru(xc, w_gate, b_gate, lam, rev):
    bsz, t_total, _ = xc.shape
    tb = TIME_BLOCK
    nt = t_total // tb
    tidx = (lambda t: nt - 1 - t) if rev else (lambda t: t)
    blk = pl.BlockSpec((1, tb, WIDTH), lambda b, t: (b, tidx(t), 0))
    return pl.pallas_call(
        functools.partial(_lru_kernel, rev=rev, tb=tb),
        grid=(bsz, nt),
        in_specs=[blk, _const_spec(w_gate.shape), _const_spec(b_gate.shape), _const_spec(lam.shape)],
        out_specs=blk, out_shape=jax.ShapeDtypeStruct((bsz, t_total, WIDTH), F32),
        scratch_shapes=[pltpu.VMEM((1, WIDTH), F32),
                        pltpu.VMEM((WIDTH // HEAD_DIM, tb, HEAD_DIM), F32),
                        pltpu.VMEM((WIDTH // HEAD_DIM, tb, HEAD_DIM), F32)],
        compiler_params=_params(2), name="rglru_rev" if rev else "rglru_fwd",
    )(xc, w_gate, b_gate, lam)


def _head_rms(o, w, gate):
    parts = []
    for h in range(N_HEADS):
        seg = o[:, h * HEAD_DIM:(h + 1) * HEAD_DIM]
        parts.append(seg * lax.rsqrt(jnp.mean(seg * seg, -1, keepdims=True) + RMS_EPS))
    return jnp.concatenate(parts, axis=1) * w * _silu(gate)


def _merge_kernel(x_ref, dnf_ref, dnb_ref, hgf_ref, hgb_ref, luf_ref, lub_ref, wc_ref, dnw_ref, hgw_ref,
                  wb_ref, wo_ref, g_ref, b_ref, o_ref, *, alpha):
    x = x_ref[...]
    hc = _dot(_bf(x), wc_ref[...])
    o_dn = _head_rms(dnf_ref[...] + dnb_ref[...], dnw_ref[...], hc[:, 0:WIDTH])
    o_hg = _head_rms(hgf_ref[...] + hgb_ref[...], hgw_ref[...], hc[:, WIDTH:2 * WIDTH])
    o_lru = (luf_ref[...] + lub_ref[...]) * jax.nn.gelu(hc[:, 2 * WIDTH:3 * WIDTH])
    mixed = None
    for n, ob in enumerate((o_dn, o_hg, o_lru)):
        lo = 3 * WIDTH + n * D_MODEL
        term = jax.nn.sigmoid(hc[:, lo:lo + D_MODEL]) * _dot(_bf(ob), wb_ref[n])
        mixed = term if mixed is None else mixed + term
    mix = _dot(_bf(mixed), wo_ref[...])
    o_ref[...] = _layer_norm(alpha * x + mix, g_ref[...], b_ref[...])


def _merge(x, branches, w_c, dn_w, hg_w, w_branch, w_out, g, b, alpha):
    n = x.shape[0]
    tm = TOKEN_BLOCK
    row = lambda width: pl.BlockSpec((tm, width), lambda i: (i, 0))
    return pl.pallas_call(
        functools.partial(_merge_kernel, alpha=alpha),
        grid=(n // tm,),
        in_specs=[row(D_MODEL)] + [row(WIDTH)] * 6
                 + [_const_spec(a.shape) for a in (w_c, dn_w, hg_w, w_branch, w_out, g, b)],
        out_specs=row(D_MODEL), out_shape=jax.ShapeDtypeStruct((n, D_MODEL), F32),
        compiler_params=_params(1), name="merge",
    )(x, *branches, w_c, dn_w, hg_w, w_branch, w_out, g, b)


def _ffn_kernel(x_ref, p_ref, w1_ref, w2_ref, wg_ref, wp_ref, g_ref, b_ref, o_ref, *, alpha):
    x = x_ref[...]
    xb = _bf(x)
    hid = jnp.square(jnp.maximum(_dot(xb, w1_ref[...]), 0.0))
    ff = _dot(_bf(hid), w2_ref[...])
    ple = jax.nn.sigmoid(_dot(xb, wg_ref[...])) * _dot(_bf(p_ref[...]), wp_ref[...])
    o_ref[...] = _layer_norm(alpha * x + ff + ple, g_ref[...], b_ref[...])


def _ffn(x, p, layer, w1, w2, wg, wp, g, b, alpha):
    n = x.shape[0]
    tm = TOKEN_BLOCK
    row = lambda width: pl.BlockSpec((tm, width), lambda i: (i, 0))
    return pl.pallas_call(
        functools.partial(_ffn_kernel, alpha=alpha),
        grid=(n // tm,),
        in_specs=[row(D_MODEL), pl.BlockSpec((None, tm, D_PLE), lambda i: (layer, i, 0))]
                 + [_const_spec(a.shape) for a in (w1, w2, wg, wp, g, b)],
        out_specs=row(D_MODEL), out_shape=jax.ShapeDtypeStruct((n, D_MODEL), F32),
        compiler_params=_params(1), name="ffn",
    )(x, p, w1, w2, wg, wp, g, b)


def _split_w_in(w):
    c = 0
    offs = {}
    for name, width in (("dqkv", 3 * WIDTH), ("dz", WIDTH), ("dab", 4 * N_HEADS), ("hq_f_f_i", 4 * WIDTH),
                        ("hgate", WIDTH), ("cx", WIDTH), ("cgate", WIDTH), ("gates", 3 * D_MODEL)):
        offs[name] = (c, c + width)
        c += width
    assert c == w.shape[1]
    cols = lambda name: w[:, offs[name][0]:offs[name][1]]
    w_conv = jnp.concatenate([cols("dqkv"), cols("cx")], axis=1)
    w_narrow = jnp.pad(cols("dab"), ((0, 0), (0, NARROW - 4 * N_HEADS)))
    w_tok = jnp.concatenate([cols("dz"), cols("hgate"), cols("cgate"), cols("gates")], axis=1)
    return _bf(w_conv), _bf(cols("hq_f_f_i")), _bf(w_narrow), _bf(w_tok)


def _block_diag(w):
    nb, d, _ = w.shape
    return jnp.einsum("kij,km->kimj", w, jnp.eye(nb, dtype=w.dtype)).reshape(nb * d, nb * d)


def _lane_pad(v):
    v = v.reshape(1, -1)
    return jnp.pad(v, ((0, 0), (0, NARROW - v.shape[1])))


def _trunk(x, p, W, depth):
    bsz, t_total, _ = x.shape
    n = bsz * t_total
    alpha = (2.0 * depth) ** 0.25
    x = x.reshape(n, D_MODEL)
    p = p.reshape(depth, n, D_PLE)
    row = lambda v: v.reshape(1, -1)
    for l in range(depth):
        w_conv, w_hg, w_narrow, w_tok = _split_w_in(W["w_in"][l])
        conv_w = jnp.concatenate([W["dn_conv_w"][l], W["lru_conv_w"][l]], axis=1)
        outs = _in_proj(x, row(W["emb_ln_g"]), row(W["emb_ln_b"]), w_conv, w_hg, w_narrow, conv_w,
                        row(W["lru_conv_b"][l]), l == 0, t_total)
        if l == 0:
            x = outs[0]
        h_hg, h_narrow, qkv, xc = [o.reshape(bsz, t_total, -1) for o in outs[-4:]]
        alog = _lane_pad(W["dn_A_log"][l])
        dtb = _lane_pad(W["dn_dt_bias"][l])
        branches = list(_deltanet(qkv, h_narrow, alog, dtb))
        for rev in (False, True):
            branches.append(_hgrn2(h_hg, W["hg_lb_logits"], l, rev))
        for d, rev in enumerate((False, True)):
            w_gate = _bf(jnp.concatenate([_block_diag(W["lru_wa"][l, d]), _block_diag(W["lru_wx"][l, d])], axis=1))
            b_gate = jnp.concatenate([W["lru_ba"][l, d], W["lru_bx"][l, d]]).reshape(1, -1)
            branches.append(_rglru(xc, w_gate, b_gate, row(W["lru_lambda"][l, d]), rev))
        branches = [o.reshape(n, WIDTH) for o in branches]
        x = _merge(x, branches, w_tok, row(jnp.tile(W["dn_norm_w"][l], N_HEADS)),
                   row(jnp.tile(W["hg_norm_w"][l], N_HEADS)), _bf(W["w_branch"][l]), _bf(W["w_out"][l]),
                   row(W["ln1_g"][l]), row(W["ln1_b"][l]), alpha)
        x = _ffn(x, p, l, _bf(W["w_mlp1"][l]), _bf(W["w_mlp2"][l]), _bf(W["w_ple_gate"][l]),
                 _bf(W["w_ple_proj"][l]), row(W["ln2_g"][l]), row(W["ln2_b"][l]), alpha)
    return x.reshape(bsz, t_total, D_MODEL)


def kernel(x_prompt, x_sample, p_prompt, p_sample, emb_ln_g, emb_ln_b, w_in, dn_conv_w, dn_A_log, dn_dt_bias,
           dn_norm_w, hg_lb_logits, hg_norm_w, lru_conv_w, lru_conv_b, lru_wa, lru_ba, lru_wx, lru_bx, lru_lambda,
           w_branch, w_out, ln1_g, ln1_b, ln2_g, ln2_b, w_mlp1, w_mlp2, w_ple_gate, w_ple_proj):
    W = dict(emb_ln_g=emb_ln_g, emb_ln_b=emb_ln_b, w_in=w_in, dn_conv_w=dn_conv_w, dn_A_log=dn_A_log,
             dn_dt_bias=dn_dt_bias, dn_norm_w=dn_norm_w, hg_lb_logits=hg_lb_logits, hg_norm_w=hg_norm_w,
             lru_conv_w=lru_conv_w, lru_conv_b=lru_conv_b, lru_wa=lru_wa, lru_ba=lru_ba, lru_wx=lru_wx,
             lru_bx=lru_bx, lru_lambda=lru_lambda, w_branch=w_branch, w_out=w_out, ln1_g=ln1_g, ln1_b=ln1_b,
             ln2_g=ln2_g, ln2_b=ln2_b, w_mlp1=w_mlp1, w_mlp2=w_mlp2, w_ple_gate=w_ple_gate, w_ple_proj=w_ple_proj)
    depth = w_in.shape[0]
    return (_trunk(x_prompt, p_prompt, W, depth), _trunk(x_sample, p_sample, W, depth))
```

```python
import functools

import jax
import jax.numpy as jnp
from jax import lax
from jax.experimental import pallas as pl
from jax.experimental.pallas import tpu as pltpu

F32 = jnp.float32
BF16 = jnp.bfloat16

D_MODEL = 1024
D_PLE = 256
N_HEADS = 4
HEAD_DIM = 128
WIDTH = N_HEADS * HEAD_DIM
CHUNK = 64
SUB = 8
FAST_SUB = 32
FAST_MIN_LOG_DECAY = -60.0
LB_FLOOR = 1e-30
LRU_BLOCKS = 8
LRU_C = 8.0
CONV_WIDTH = 4
D_FF = 4 * D_MODEL
LN_EPS = 1e-5
RMS_EPS = 1e-6
L2_EPS = 1e-6
SUBLANES = 8
HALO = 8
NARROW = 128

VMEM_LIMIT_BYTES = 56 * 1024 * 1024
TOKEN_BLOCK = 512
IN_PROJ_SUBBLOCKS = 2
TIME_BLOCK = 512
SCAN_BLOCK = 1024
DN_ROWS = 2


def _dot(a, b):
    return jnp.dot(a, b, preferred_element_type=F32)


def _dot_nt(a, b):
    return lax.dot_general(a, b, (((1,), (1,)), ((), ())), preferred_element_type=F32)


def _bf(x):
    return x.astype(BF16)


def _layer_norm(x, g, b):
    mu = jnp.mean(x, -1, keepdims=True)
    xc = x - mu
    var = jnp.mean(xc * xc, -1, keepdims=True)
    return xc * lax.rsqrt(var + LN_EPS) * g + b


def _softplus(x):
    return jnp.maximum(x, 0.0) + jnp.log1p(jnp.exp(-jnp.abs(x)))


def _log_sigmoid(x):
    return jnp.minimum(x, 0.0) - jnp.log1p(jnp.exp(-jnp.abs(x)))


def _silu(x):
    return x * jax.nn.sigmoid(x)


def _const_spec(shape):
    zeros = (0,) * len(shape)
    return pl.BlockSpec(shape, lambda *_: zeros, pipeline_mode=pl.Buffered(1))


def _params(n_axes):
    return pltpu.CompilerParams(dimension_semantics=("arbitrary",) * n_axes,
                                vmem_limit_bytes=VMEM_LIMIT_BYTES)


def _in_proj_kernel(x_ref, xp_ref, xn_ref, g_ref, b_ref, wc_ref, wh_ref, wn_ref, cw_ref, cb_ref, *refs,
                    apply_ln, blocks_per_seq):
    n_sub = IN_PROJ_SUBBLOCKS
    if apply_ln:
        xo_ref, hg_ref, hn_ref, qkv_ref, xc_ref = refs[:5]
    else:
        hg_ref, hn_ref, qkv_ref, xc_ref = refs[:4]
    xe_refs = refs[-n_sub:]
    tm = x_ref.shape[0]
    sub = tm // n_sub
    pos = pl.program_id(0) % blocks_per_seq

    def project(j):
        r0 = j * sub
        rows = slice(r0, r0 + sub)
        before = x_ref[r0 - HALO:r0, :] if j > 0 else xp_ref[...]
        after = x_ref[r0 + sub:r0 + sub + HALO, :] if j < n_sub - 1 else xn_ref[...]
        x_ext = jnp.concatenate([before, x_ref[rows, :], after], axis=0)
        if apply_ln:
            x_ext = _layer_norm(x_ext, g_ref[...], b_ref[...])
            xo_ref[rows, :] = x_ext[HALO:HALO + sub]
        xb = _bf(x_ext[HALO:HALO + sub])
        hg_ref[rows, :] = _dot(xb, wh_ref[...])
        hn_ref[rows, :] = _dot(xb, wn_ref[...])
        h_ext = _dot(_bf(x_ext), wc_ref[...])
        in_seq_before = True if j > 0 else pos > 0
        in_seq_after = True if j < n_sub - 1 else pos < blocks_per_seq - 1
        xe_refs[j][0:HALO, :] = jnp.where(in_seq_before, h_ext[0:HALO], 0.0)
        xe_refs[j][HALO:HALO + sub, :] = h_ext[HALO:HALO + sub]
        xe_refs[j][HALO + sub:, :] = jnp.where(in_seq_after, h_ext[HALO + sub:], 0.0)

    def finish(j):
        rows = slice(j * sub, (j + 1) * sub)
        conv = _conv4(xe_refs[j], cw_ref[...], sub, 0)
        xc_ref[rows, :] = conv[:, 3 * WIDTH:] + cb_ref[...]
        qkv = _silu(conv[:, :3 * WIDTH])
        for h in range(N_HEADS):
            lo = h * HEAD_DIM
            qh = qkv[:, lo:lo + HEAD_DIM]
            kh = qkv[:, WIDTH + lo:WIDTH + lo + HEAD_DIM]
            qkv_ref[rows, lo:lo + HEAD_DIM] = _bf(qh * (lax.rsqrt(jnp.sum(qh * qh, -1, keepdims=True) + L2_EPS)
                                                        * (HEAD_DIM ** -0.5)))
            qkv_ref[rows, WIDTH + lo:WIDTH + lo + HEAD_DIM] = _bf(
                kh * lax.rsqrt(jnp.sum(kh * kh, -1, keepdims=True) + L2_EPS))
        qkv_ref[rows, 2 * WIDTH:3 * WIDTH] = _bf(qkv[:, 2 * WIDTH:3 * WIDTH])

    project(0)
    for j in range(1, IN_PROJ_SUBBLOCKS):
        project(j)
        finish(j - 1)
    finish(IN_PROJ_SUBBLOCKS - 1)


def _in_proj(x, g, b, w_conv, w_hg, w_narrow, conv_w, conv_b, apply_ln, t_total):
    n = x.shape[0]
    tm = TOKEN_BLOCK
    per = tm // HALO
    last = n // HALO - 1
    row = lambda width: pl.BlockSpec((tm, width), lambda i: (i, 0))
    widths = [(w_hg.shape[1], F32), (NARROW, F32), (3 * WIDTH, BF16), (WIDTH, F32)]
    if apply_ln:
        widths = [(D_MODEL, F32)] + widths
    out_specs = [row(w) for w, _ in widths]
    out_shape = [jax.ShapeDtypeStruct((n, w), dt) for w, dt in widths]
    return pl.pallas_call(
        functools.partial(_in_proj_kernel, apply_ln=apply_ln, blocks_per_seq=t_total // tm),
        grid=(n // tm,),
        in_specs=[row(D_MODEL),
                  pl.BlockSpec((HALO, D_MODEL), lambda i: (jnp.maximum(i * per - 1, 0), 0)),
                  pl.BlockSpec((HALO, D_MODEL), lambda i: (jnp.minimum((i + 1) * per, last), 0))]
                 + [_const_spec(a.shape) for a in (g, b, w_conv, w_hg, w_narrow, conv_w, conv_b)],
        out_specs=out_specs, out_shape=out_shape,
        scratch_shapes=[pltpu.VMEM((tm // IN_PROJ_SUBBLOCKS + 2 * HALO, w_conv.shape[1]), F32)] * IN_PROJ_SUBBLOCKS,
        compiler_params=_params(1), name="in_proj",
    )(x, x, x, g, b, w_conv, w_hg, w_narrow, conv_w, conv_b)


def _conv4(xe_ref, cw, rows, r0):
    acc = cw[0:1, :] * xe_ref[pl.ds(r0 + HALO - 1, rows), :]
    for j in range(1, CONV_WIDTH):
        acc = acc + cw[j:j + 1, :] * xe_ref[pl.ds(r0 + HALO - 1 + j, rows), :]
    return acc


def _chunk_cumsum(x, rev, window=None):
    tb = x.shape[0]
    rc = lax.broadcasted_iota(jnp.int32, x.shape, 0) % CHUNK
    s = 1
    windowed = None
    while s < CHUNK:
        if s == window:
            windowed = x
        if rev:
            x = x + jnp.where(rc < CHUNK - s, pltpu.roll(x, tb - s, 0), 0.0)
        else:
            x = x + jnp.where(rc >= s, pltpu.roll(x, s, 0), 0.0)
        s *= 2
    return x if window is None else (x, windowed)


def _dn_kernel(qkv_f_ref, ab_f_ref, qkv_r_ref, ab_r_ref, alog_ref, dtb_ref, of_ref, or_ref,
               gc_s, beta_s, st_ref, *, tb, nb):
    @pl.when(pl.program_id(1) == 0)
    def _():
        st_ref[...] = jnp.zeros_like(st_ref)

    streams = [(rev, bi) for rev in (False, True) for bi in range(nb)]
    for s, (rev, bi) in enumerate(streams):
        ab = (ab_r_ref if rev else ab_f_ref)[bi]
        g = -jnp.exp(alog_ref[...]) * _softplus(ab + dtb_ref[...])
        beta_s[s] = jax.nn.sigmoid(ab)
        gc_s[s] = _chunk_cumsum(g, rev)

    ii = lax.broadcasted_iota(jnp.int32, (CHUNK, CHUNK), 0)
    jj = lax.broadcasted_iota(jnp.int32, (CHUNK, CHUNK), 1)
    incl = {False: ii >= jj, True: ii <= jj}
    strict = {False: ii > jj, True: ii < jj}
    eye = (ii == jj).astype(F32)
    nc = tb // CHUNK
    units = [(s, h) for s in range(len(streams)) for h in range(N_HEADS)]
    n_u = len(units)
    n_doublings = CHUNK.bit_length() - 2

    def chunk(c, carry):
        rows = {False: pl.ds(pl.multiple_of(c * CHUNK, CHUNK), CHUNK),
                True: pl.ds(pl.multiple_of((nc - 1 - c) * CHUNK, CHUNK), CHUNK)}
        gc = [gc_s[s, rows[rev], :] for s, (rev, _) in enumerate(streams)]
        gct = [x.T for x in gc]
        bt = [beta_s[s, rows[rev], :] for s, (rev, _) in enumerate(streams)]
        gcol, bcol, q, k, k_bf, v, dec, kb, eg, glast = [], [], [], [], [], [], [], [], [], []
        for s, h in units:
            rev = streams[s][0]
            lo = h * HEAD_DIM
            lane_g = (N_HEADS if rev else 0) + h
            lane_b = 2 * N_HEADS + lane_g
            last = 0 if rev else CHUNK - 1
            gcol.append(gc[s][:, lane_g:lane_g + 1])
            glast.append(gc[s][last:last + 1, lane_g:lane_g + 1])
            bcol.append(bt[s][:, lane_b:lane_b + 1])
            grow = gct[s][lane_g:lane_g + 1, :]
            qkv_ref, bi = (qkv_r_ref if rev else qkv_f_ref), streams[s][1]
            k_bf.append(qkv_ref[bi, rows[rev], WIDTH + lo:WIDTH + lo + HEAD_DIM])
            q.append(qkv_ref[bi, rows[rev], lo:lo + HEAD_DIM].astype(F32))
            k.append(k_bf[-1].astype(F32))
            v.append(qkv_ref[bi, rows[rev], 2 * WIDTH + lo:2 * WIDTH + lo + HEAD_DIM].astype(F32))
            dec.append(jnp.where(incl[rev], jnp.exp(jnp.where(incl[rev], gcol[-1] - grow, 0.0)), 0.0))
            kb.append(k[-1] * bcol[-1])
            eg.append(jnp.exp(gcol[-1]))
        qk = [_dot_nt(_bf(jnp.concatenate([q[u], kb[u]], axis=0)), k_bf[u]) for u in range(n_u)]
        attn = [qk[u][:CHUNK] * dec[u] for u in range(n_u)]
        pw = [-jnp.where(strict[streams[s][0]], qk[u][CHUNK:] * dec[u], 0.0) for u, (s, _) in enumerate(units)]
        tinv = [eye + pw[u] for u in range(n_u)]
        pw = [_bf(x) for x in pw]
        pw = [_bf(_dot(x, x)) for x in pw]
        for step in range(n_doublings):
            for u in range(n_u):
                tinv[u] = tinv[u] + _dot(_bf(tinv[u]), pw[u])
                if step < n_doublings - 1:
                    pw[u] = _bf(_dot(pw[u], pw[u]))
        uw = [_dot(_bf(tinv[u]), _bf(jnp.concatenate([v[u] * bcol[u], kb[u] * eg[u]], axis=1)))
              for u in range(n_u)]
        s_old = [st_ref[s, h] for s, h in units]
        wq = [_dot(_bf(jnp.concatenate([uw[u][:, HEAD_DIM:], q[u] * eg[u]], axis=0)), _bf(s_old[u]))
              for u in range(n_u)]
        v_new = [uw[u][:, :HEAD_DIM] - wq[u][:CHUNK] for u in range(n_u)]
        for u, (s, h) in enumerate(units):
            rev, bi = streams[s]
            lo = h * HEAD_DIM
            o_ref = or_ref if rev else of_ref
            o_ref[bi, rows[rev], lo:lo + HEAD_DIM] = wq[u][CHUNK:] + _dot(_bf(attn[u]), _bf(v_new[u]))
            kdec = k[u] * jnp.exp(glast[u] - gcol[u])
            st_ref[s, h] = s_old[u] * jnp.exp(glast[u]) + _dot(_bf(kdec.T), _bf(v_new[u]))
        return carry

    lax.fori_loop(0, nc, chunk, 0)


def _deltanet(qkv, h_narrow, alog_vec, dtb_vec):
    bsz, t_total, _ = qkv.shape
    tb = TIME_BLOCK
    nb = DN_ROWS
    nt = t_total // tb
    fwd = lambda width: pl.BlockSpec((nb, tb, width), lambda b, t: (b, t, 0))
    rev = lambda width: pl.BlockSpec((nb, tb, width), lambda b, t: (b, nt - 1 - t, 0))
    o_shape = jax.ShapeDtypeStruct((bsz, t_total, WIDTH), F32)
    n_streams = 2 * nb
    return pl.pallas_call(
        functools.partial(_dn_kernel, tb=tb, nb=nb),
        grid=(bsz // nb, nt),
        in_specs=[fwd(3 * WIDTH), fwd(NARROW), rev(3 * WIDTH), rev(NARROW),
                  _const_spec((1, NARROW)), _const_spec((1, NARROW))],
        out_specs=[fwd(WIDTH), rev(WIDTH)], out_shape=[o_shape, o_shape],
        scratch_shapes=[pltpu.VMEM((n_streams, tb, NARROW), F32), pltpu.VMEM((n_streams, tb, NARROW), F32),
                        pltpu.VMEM((n_streams, N_HEADS, HEAD_DIM, HEAD_DIM), F32)],
        compiler_params=_params(2), name="deltanet",
    )(qkv, h_narrow, qkv, h_narrow, alog_vec, dtb_vec)


def _hg_kernel(qf_ref, ff_ref, if_ref, qr_ref, fr_ref, ir_ref, lg_ref, of_ref, or_ref,
               bf_s, kf_s, stf_ref, br_s, kr_s, str_ref, *, tb, nt, layer):
    _hg_direction(qf_ref, ff_ref, if_ref, lg_ref, of_ref, bf_s, kf_s, stf_ref, rev=False, tb=tb, nt=nt, layer=layer)
    _hg_direction(qr_ref, fr_ref, ir_ref, lg_ref, or_ref, br_s, kr_s, str_ref, rev=True, tb=tb, nt=nt, layer=layer)


def _hg_direction(q_ref, f_ref, i_ref, lg_ref, o_ref, b_s, k_s, st_ref, *, rev, tb, nt, layer):
    @pl.when(pl.program_id(1) == 0)
    def _():
        st_ref[...] = jnp.zeros_like(st_ref)

    lg = lg_ref[...]
    ex = jnp.exp(lg - jnp.max(lg, axis=0, keepdims=True))
    sm = ex / jnp.sum(ex, axis=0, keepdims=True)
    lb = jnp.maximum(jnp.sum(sm[0:layer + 1], axis=0, keepdims=True) - sm[0:1], 0.0)
    lb_floor = jnp.maximum(lb, LB_FLOOR)

    nc = tb // CHUNK
    def gates(c, carry):
        rows = pl.ds(pl.multiple_of(c * CHUNK, CHUNK), CHUNK)
        fz = f_ref[0, rows, :]
        t = jnp.exp(-jnp.abs(fz))
        pos = fz >= 0.0
        inv = 1.0 / (1.0 + t)
        log_f = jnp.log(jnp.where(pos, 1.0 + lb_floor * t, t + lb_floor)) + jnp.log(inv)
        k_s[rows, :] = (1.0 - lb) * (jnp.where(pos, t, 1.0) * inv)
        b, win = _chunk_cumsum(log_f, rev, FAST_SUB)
        b_s[rows, :] = b
        return jnp.minimum(carry, jnp.min(win, axis=0, keepdims=True))

    min_win = lax.fori_loop(0, nc, gates, jnp.zeros((1, WIDTH), F32))

    sub_i = lax.broadcasted_iota(jnp.int32, (SUB, HEAD_DIM), 0)
    lane_c = lax.broadcasted_iota(jnp.int32, (SUB, CHUNK), 1)
    fast_i = lax.broadcasted_iota(jnp.int32, (FAST_SUB, CHUNK), 0)
    fast_c = lax.broadcasted_iota(jnp.int32, (FAST_SUB, CHUNK), 1)
    last = 0 if rev else CHUNK - 1

    def pad_rows(x, lo):
        hi = lo + x.shape[0]
        return jnp.concatenate(([jnp.zeros((lo, HEAD_DIM), F32)] if lo else []) + [x]
                               + ([jnp.zeros((CHUNK - hi, HEAD_DIM), F32)] if hi < CHUNK else []), axis=0)

    def boundary(b, a0, sub):
        if rev:
            return b[a0 + sub:a0 + sub + 1] if a0 + sub < CHUNK else 0.0
        return b[a0 - 1:a0] if a0 > 0 else 0.0

    def scores_exact(b, q, k):
        a_rows = []
        for a0 in range(0, CHUNK, SUB):
            b_blk = b[a0:a0 + SUB]
            q_blk = q[a0:a0 + SUB]
            p_lo, p_hi = (a0 + SUB, CHUNK) if rev else (0, a0)
            if p_hi > p_lo:
                bs = boundary(b, a0, SUB)
                k_past = pad_rows(k[p_lo:p_hi] * jnp.exp(bs - b[p_lo:p_hi]), p_lo)
                p = _dot_nt(_bf(q_blk * jnp.exp(b_blk - bs)), _bf(k_past))
            else:
                p = jnp.zeros((SUB, CHUNK), F32)
            for j in range(SUB):
                valid = (sub_i <= j) if rev else (sub_i >= j)
                e = jnp.exp(b_blk - b[a0 + j:a0 + j + 1])
                col = jnp.sum(jnp.where(valid, q_blk * k[a0 + j:a0 + j + 1] * e, 0.0), -1, keepdims=True)
                p = jnp.where(lane_c == a0 + j, col, p)
            a_rows.append(p)
        return jnp.concatenate(a_rows, axis=0)

    def scores_fast(b, q, k):
        a_rows = []
        for a0 in range(0, CHUNK, FAST_SUB):
            bs = boundary(b, a0, FAST_SUB)
            k_lo, k_hi = (a0, CHUNK) if rev else (0, a0 + FAST_SUB)
            k_seen = pad_rows(k[k_lo:k_hi] * jnp.exp(bs - b[k_lo:k_hi]), k_lo)
            p = _dot_nt(_bf(q[a0:a0 + FAST_SUB] * jnp.exp(b[a0:a0 + FAST_SUB] - bs)), _bf(k_seen))
            causal = (fast_c >= fast_i + a0) if rev else (fast_c <= fast_i + a0)
            a_rows.append(jnp.where(causal, p, 0.0))
        return jnp.concatenate(a_rows, axis=0)

    def load(r0, h):
        cols = slice(h * HEAD_DIM, (h + 1) * HEAD_DIM)
        rows = pl.ds(r0, CHUNK)
        return b_s[rows, cols], q_ref[0, rows, cols], k_s[rows, cols], i_ref[0, rows, cols]

    def exact_chunk(c, carry):
        cc = (nc - 1 - c) if rev else c
        r0 = pl.multiple_of(cc * CHUNK, CHUNK)
        for h in range(N_HEADS):
            b, q, k, v = load(r0, h)
            blast = b[last:last + 1]
            st = st_ref[h]
            o = _dot_nt(_bf(q * jnp.exp(b)), _bf(st))
            st_ref[h] = st * jnp.exp(blast) + _dot(_bf(v.T), _bf(k * jnp.exp(blast - b)))
            a = scores_exact(b, q, k)
            o_ref[0, pl.ds(r0, CHUNK), h * HEAD_DIM:(h + 1) * HEAD_DIM] = o + _dot(_bf(a), _bf(v))
        return carry

    def fast_block():
        order = list(range(nc - 1, -1, -1)) if rev else list(range(nc))
        units = [(c, h) for c in order for h in range(N_HEADS)]
        data = {u: load(u[0] * CHUNK, u[1]) for u in units}
        q_dec, d_last, upd = {}, {}, {}
        for u in units:
            b, q, k, v = data[u]
            blast = b[last:last + 1]
            q_dec[u] = _bf(q * jnp.exp(b))
            d_last[u] = jnp.exp(blast)
            upd[u] = _dot(_bf(v.T), _bf(k * jnp.exp(blast - b)))
        inter = {}
        for h in range(N_HEADS):
            st = st_ref[h]
            for c in order:
                u = (c, h)
                inter[u] = _dot_nt(q_dec[u], _bf(st))
                st = st * d_last[u] + upd[u]
            st_ref[h] = st
        intra = {u: scores_fast(*data[u][:3]) for u in units}
        for c, h in units:
            o_ref[0, c * CHUNK:(c + 1) * CHUNK, h * HEAD_DIM:(h + 1) * HEAD_DIM] = (
                inter[c, h] + _dot(_bf(intra[c, h]), _bf(data[c, h][3])))

    bounded = jnp.min(min_win) > FAST_MIN_LOG_DECAY

    @pl.when(bounded)
    def _():
        fast_block()

    @pl.when(jnp.logical_not(bounded))
    def _():
        lax.fori_loop(0, nc, exact_chunk, 0)


def _hgrn2(h_wide, lb_logits, layer):
    bsz, t_total, _ = h_wide.shape
    tb = SCAN_BLOCK
    nt = t_total // tb
    fwd = lambda c: pl.BlockSpec((1, tb, WIDTH), lambda b, t: (b, t, c))
    rev = lambda c: pl.BlockSpec((1, tb, WIDTH), lambda b, t: (b, nt - 1 - t, c))
    o_shape = jax.ShapeDtypeStruct((bsz, t_total, WIDTH), F32)
    per_direction = [pltpu.VMEM((tb, WIDTH), F32), pltpu.VMEM((tb, WIDTH), F32),
                     pltpu.VMEM((N_HEADS, HEAD_DIM, HEAD_DIM), F32)]
    return pl.pallas_call(
        functools.partial(_hg_kernel, tb=tb, nt=nt, layer=layer),
        grid=(bsz, nt),
        in_specs=[fwd(0), fwd(1), fwd(3), rev(0), rev(2), rev(3), _const_spec(lb_logits.shape)],
        out_specs=[fwd(0), rev(0)], out_shape=[o_shape, o_shape],
        scratch_shapes=per_direction * 2,
        compiler_params=_params(2), name="hgrn2",
    )(h_wide, h_wide, h_wide, h_wide, h_wide, h_wide, lb_logits)


def _linear_scan(a, h, axis, rev):
    n = a.shape[axis]
    idx = lax.broadcasted_iota(jnp.int32, a.shape, axis)
    s = 1
    while s < n:
        valid = (idx < n - s) if rev else (idx >= s)
        shift = (n - s) if rev else s
        h = jnp.where(valid, a * pltpu.roll(h, shift, axis) + h, h)
        a = jnp.where(valid, a * pltpu.roll(a, shift, axis), a)
        s *= 2
    return a, h


def _lru_kernel(xcf_ref, xcr_ref, wgf_ref, bgf_ref, lamf_ref, wgr_ref, bgr_ref, lamr_ref, of_ref, or_ref,
                carryf_ref, af_s, hf_s, carryr_ref, ar_s, hr_s, *, tb):
    _lru_direction(xcf_ref, wgf_ref, bgf_ref, lamf_ref, of_ref, carryf_ref, af_s, hf_s, rev=False, tb=tb)
    _lru_direction(xcr_ref, wgr_ref, bgr_ref, lamr_ref, or_ref, carryr_ref, ar_s, hr_s, rev=True, tb=tb)


def _lru_direction(xc_ref, wg_ref, bg_ref, lam_ref, o_ref, carry_ref, a_s, h_s, *, rev, tb):
    @pl.when(pl.program_id(1) == 0)
    def _():
        carry_ref[...] = jnp.zeros_like(carry_ref)

    xc = xc_ref[0]
    gates = jax.nn.sigmoid(_dot(_bf(xc), wg_ref[...]) + bg_ref[...])
    log_a = LRU_C * gates[:, :WIDTH] * _log_sigmoid(lam_ref[...])
    a = jnp.exp(log_a)
    hval = jnp.sqrt(jnp.maximum(1.0 - a * a, 0.0)) * (gates[:, WIDTH:] * xc)

    ng = tb // SUBLANES
    a, hval = _linear_scan(a.reshape(ng, SUBLANES, WIDTH), hval.reshape(ng, SUBLANES, WIDTH), 1, rev)
    a = a.reshape(tb, WIDTH)
    hval = hval.reshape(tb, WIDTH)
    n_tiles = WIDTH // HEAD_DIM
    for j in range(n_tiles):
        a_s[j] = a[:, j * HEAD_DIM:(j + 1) * HEAD_DIM]
        h_s[j] = hval[:, j * HEAD_DIM:(j + 1) * HEAD_DIM]
    edge = 0 if rev else SUBLANES - 1
    totals = lambda ref: jnp.concatenate([ref[j, pl.ds(edge, ng, stride=SUBLANES), :] for j in range(n_tiles)],
                                         axis=1)
    a_grp, h_grp = _linear_scan(totals(a_s), totals(h_s), 0, rev)
    carry = carry_ref[...]
    state = h_grp + a_grp * carry
    g = lax.broadcasted_iota(jnp.int32, (ng, WIDTH), 0)
    if rev:
        carry_in = jnp.where(g < ng - 1, pltpu.roll(state, ng - 1, 0), carry)
        carry_ref[...] = state[0:1]
    else:
        carry_in = jnp.where(g >= 1, pltpu.roll(state, 1, 0), carry)
        carry_ref[...] = state[ng - 1:ng]
    for gi in range(ng):
        rows = slice(gi * SUBLANES, (gi + 1) * SUBLANES)
        for j in range(n_tiles):
            lanes = slice(j * HEAD_DIM, (j + 1) * HEAD_DIM)
            o_ref[0, rows, lanes] = h_s[j, rows, :] + a_s[j, rows, :] * carry_in[gi:gi + 1, lanes]


def _rglru(xc, gate_params):
    bsz, t_total, _ = xc.shape
    tb = SCAN_BLOCK
    nt = t_total // tb
    fwd = pl.BlockSpec((1, tb, WIDTH), lambda b, t: (b, t, 0))
    rev = pl.BlockSpec((1, tb, WIDTH), lambda b, t: (b, nt - 1 - t, 0))
    o_shape = jax.ShapeDtypeStruct((bsz, t_total, WIDTH), F32)
    params = [a for triple in gate_params for a in triple]
    per_direction = [pltpu.VMEM((1, WIDTH), F32),
                     pltpu.VMEM((WIDTH // HEAD_DIM, tb, HEAD_DIM), F32),
                     pltpu.VMEM((WIDTH // HEAD_DIM, tb, HEAD_DIM), F32)]
    return pl.pallas_call(
        functools.partial(_lru_kernel, tb=tb),
        grid=(bsz, nt),
        in_specs=[fwd, rev] + [_const_spec(a.shape) for a in params],
        out_specs=[fwd, rev], out_shape=[o_shape, o_shape],
        scratch_shapes=per_direction * 2,
        compiler_params=_params(2), name="rglru",
    )(xc, xc, *params)


def _head_rms(o, w, gate):
    parts = []
    for h in range(N_HEADS):
        seg = o[:, h * HEAD_DIM:(h + 1) * HEAD_DIM]
        parts.append(seg * lax.rsqrt(jnp.mean(seg * seg, -1, keepdims=True) + RMS_EPS))
    return jnp.concatenate(parts, axis=1) * w * _silu(gate)


def _merge_kernel(x_ref, dnf_ref, dnb_ref, hgf_ref, hgb_ref, luf_ref, lub_ref, wc_ref, dnw_ref, hgw_ref,
                  wb_ref, wo_ref, g_ref, b_ref, o_ref, *, alpha):
    x = x_ref[...]
    hc = _dot(_bf(x), wc_ref[...])
    o_dn = _head_rms(dnf_ref[...] + dnb_ref[...], dnw_ref[...], hc[:, 0:WIDTH])
    o_hg = _head_rms(hgf_ref[...] + hgb_ref[...], hgw_ref[...], hc[:, WIDTH:2 * WIDTH])
    o_lru = (luf_ref[...] + lub_ref[...]) * jax.nn.gelu(hc[:, 2 * WIDTH:3 * WIDTH])
    mixed = None
    for n, ob in enumerate((o_dn, o_hg, o_lru)):
        lo = 3 * WIDTH + n * D_MODEL
        term = jax.nn.sigmoid(hc[:, lo:lo + D_MODEL]) * _dot(_bf(ob), wb_ref[n])
        mixed = term if mixed is None else mixed + term
    mix = _dot(_bf(mixed), wo_ref[...])
    o_ref[...] = _layer_norm(alpha * x + mix, g_ref[...], b_ref[...])


def _merge(x, branches, w_c, dn_w, hg_w, w_branch, w_out, g, b, alpha):
    n = x.shape[0]
    tm = TOKEN_BLOCK
    row = lambda width: pl.BlockSpec((tm, width), lambda i: (i, 0))
    return pl.pallas_call(
        functools.partial(_merge_kernel, alpha=alpha),
        grid=(n // tm,),
        in_specs=[row(D_MODEL)] + [row(WIDTH)] * 6
                 + [_const_spec(a.shape) for a in (w_c, dn_w, hg_w, w_branch, w_out, g, b)],
        out_specs=row(D_MODEL), out_shape=jax.ShapeDtypeStruct((n, D_MODEL), F32),
        compiler_params=_params(1), name="merge",
    )(x, *branches, w_c, dn_w, hg_w, w_branch, w_out, g, b)


def _ffn_kernel(x_ref, p_ref, w1_ref, w2_ref, wg_ref, wp_ref, g_ref, b_ref, o_ref, *, alpha):
    x = x_ref[...]
    xb = _bf(x)
    hid = jnp.square(jnp.maximum(_dot(xb, w1_ref[...]), 0.0))
    ff = _dot(_bf(hid), w2_ref[...])
    ple = jax.nn.sigmoid(_dot(xb, wg_ref[...])) * _dot(_bf(p_ref[...]), wp_ref[...])
    o_ref[...] = _layer_norm(alpha * x + ff + ple, g_ref[...], b_ref[...])


def _ffn(x, p, layer, w1, w2, wg, wp, g, b, alpha):
    n = x.shape[0]
    tm = TOKEN_BLOCK
    row = lambda width: pl.BlockSpec((tm, width), lambda i: (i, 0))
    return pl.pallas_call(
        functools.partial(_ffn_kernel, alpha=alpha),
        grid=(n // tm,),
        in_specs=[row(D_MODEL), pl.BlockSpec((None, tm, D_PLE), lambda i: (layer, i, 0))]
                 + [_const_spec(a.shape) for a in (w1, w2, wg, wp, g, b)],
        out_specs=row(D_MODEL), out_shape=jax.ShapeDtypeStruct((n, D_MODEL), F32),
        compiler_params=_params(1), name="ffn",
    )(x, p, w1, w2, wg, wp, g, b)


def _split_w_in(w):
    c = 0
    offs = {}
    for name, width in (("dqkv", 3 * WIDTH), ("dz", WIDTH), ("dab", 4 * N_HEADS), ("hq_f_f_i", 4 * WIDTH),
                        ("hgate", WIDTH), ("cx", WIDTH), ("cgate", WIDTH), ("gates", 3 * D_MODEL)):
        offs[name] = (c, c + width)
        c += width
    assert c == w.shape[1]
    cols = lambda name: w[:, offs[name][0]:offs[name][1]]
    w_conv = jnp.concatenate([cols("dqkv"), cols("cx")], axis=1)
    w_narrow = jnp.pad(cols("dab"), ((0, 0), (0, NARROW - 4 * N_HEADS)))
    w_tok = jnp.concatenate([cols("dz"), cols("hgate"), cols("cgate"), cols("gates")], axis=1)
    return _bf(w_conv), _bf(cols("hq_f_f_i")), _bf(w_narrow), _bf(w_tok)


def _block_diag(w):
    nb, d, _ = w.shape
    return jnp.einsum("kij,km->kimj", w, jnp.eye(nb, dtype=w.dtype)).reshape(nb * d, nb * d)


def _lane_pad(v):
    v = v.reshape(1, -1)
    return jnp.pad(v, ((0, 0), (0, NARROW - v.shape[1])))


def _trunk(x, p, W, depth):
    bsz, t_total, _ = x.shape
    n = bsz * t_total
    alpha = (2.0 * depth) ** 0.25
    x = x.reshape(n, D_MODEL)
    p = p.reshape(depth, n, D_PLE)
    row = lambda v: v.reshape(1, -1)
    for l in range(depth):
        w_conv, w_hg, w_narrow, w_tok = _split_w_in(W["w_in"][l])
        conv_w = jnp.concatenate([W["dn_conv_w"][l], W["lru_conv_w"][l]], axis=1)
        outs = _in_proj(x, row(W["emb_ln_g"]), row(W["emb_ln_b"]), w_conv, w_hg, w_narrow, conv_w,
                        row(W["lru_conv_b"][l]), l == 0, t_total)
        if l == 0:
            x = outs[0]
        h_hg, h_narrow, qkv, xc = [o.reshape(bsz, t_total, -1) for o in outs[-4:]]
        alog = _lane_pad(W["dn_A_log"][l])
        dtb = _lane_pad(W["dn_dt_bias"][l])
        branches = list(_deltanet(qkv, h_narrow, alog, dtb))
        branches += _hgrn2(h_hg, W["hg_lb_logits"], l)
        gate_params = []
        for d in range(2):
            w_gate = _bf(jnp.concatenate([_block_diag(W["lru_wa"][l, d]), _block_diag(W["lru_wx"][l, d])], axis=1))
            b_gate = jnp.concatenate([W["lru_ba"][l, d], W["lru_bx"][l, d]]).reshape(1, -1)
            gate_params.append((w_gate, b_gate, row(W["lru_lambda"][l, d])))
        branches += _rglru(xc, gate_params)
        branches = [o.reshape(n, WIDTH) for o in branches]
        x = _merge(x, branches, w_tok, row(jnp.tile(W["dn_norm_w"][l], N_HEADS)),
                   row(jnp.tile(W["hg_norm_w"][l], N_HEADS)), _bf(W["w_branch"][l]), _bf(W["w_out"][l]),
                   row(W["ln1_g"][l]), row(W["ln1_b"][l]), alpha)
        x = _ffn(x, p, l, _bf(W["w_mlp1"][l]), _bf(W["w_mlp2"][l]), _bf(W["w_ple_gate"][l]),
                 _bf(W["w_ple_proj"][l]), row(W["ln2_g"][l]), row(W["ln2_b"][l]), alpha)
    return x.reshape(bsz, t_total, D_MODEL)


def kernel(x_prompt, x_sample, p_prompt, p_sample, emb_ln_g, emb_ln_b, w_in, dn_conv_w, dn_A_log, dn_dt_bias,
           dn_norm_w, hg_lb_logits, hg_norm_w, lru_conv_w, lru_conv_b, lru_wa, lru_ba, lru_wx, lru_bx, lru_lambda,
           w_branch, w_out, ln1_g, ln1_b, ln2_g, ln2_b, w_mlp1, w_mlp2, w_ple_gate, w_ple_proj):
    W = dict(emb_ln_g=emb_ln_g, emb_ln_b=emb_ln_b, w_in=w_in, dn_conv_w=dn_conv_w, dn_A_log=dn_A_log,
             dn_dt_bias=dn_dt_bias, dn_norm_w=dn_norm_w, hg_lb_logits=hg_lb_logits, hg_norm_w=hg_norm_w,
             lru_conv_w=lru_conv_w, lru_conv_b=lru_conv_b, lru_wa=lru_wa, lru_ba=lru_ba, lru_wx=lru_wx,
             lru_bx=lru_bx, lru_lambda=lru_lambda, w_branch=w_branch, w_out=w_out, ln1_g=ln1_g, ln1_b=ln1_b,
             ln2_g=ln2_g, ln2_b=ln2_b, w_mlp1=w_mlp1, w_mlp2=w_mlp2, w_ple_gate=w_ple_gate, w_ple_proj=w_ple_proj)
    depth = w_in.shape[0]
    return (_trunk(x_prompt, p_prompt, W, depth), _trunk(x_sample, p_sample, W, depth))
```
